```python
import math
import jax, jax.numpy as jnp
from jax import lax
import numpy as np

D_MODEL = 1024
BATCH = 8
SEQ = 4096
DEPTH = 2

HEAD_DIM = 64
N_HEADS = D_MODEL // HEAD_DIM
BRANCH_WIDTH = N_HEADS * HEAD_DIM
N_MIXERS = 2
DILATED_PAIRS = ((128, 1), (512, 4), (2048, 16))
N_DIL_GROUPS = len(DILATED_PAIRS)
T5_BUCKETS = 32
T5_MAX_DISTANCE = 1024
GRID_W = 64
NA_ROWS = 8
NA_COLS = 16
NA_COL_BLOCK = 16
NA_SLAB = NA_COL_BLOCK + NA_COLS
RMS_EPS = 1e-6
NEG_INF = -1e30
N_A_LAYERS = (DEPTH + N_MIXERS - 1) // N_MIXERS
N_B_LAYERS = DEPTH // N_MIXERS
A_IN_COLS = N_DIL_GROUPS * 3 * BRANCH_WIDTH + BRANCH_WIDTH
B_IN_COLS = 4 * BRANCH_WIDTH

kernel_name = "hybrid_dilated_neighbourhood_encoder"


def rms_norm(x, g):
    xf = x.astype(jnp.float32)
    y = xf * lax.rsqrt(jnp.mean(xf * xf, axis=-1, keepdims=True) + RMS_EPS)
    return (y * g.astype(jnp.float32)).astype(x.dtype)


def t5_bucket(rel):
    half = T5_BUCKETS // 2
    max_exact = half // 2
    ret = jnp.where(rel > 0, half, 0)
    n = jnp.abs(rel)
    nf = jnp.maximum(n, 1).astype(jnp.float32)
    large = max_exact + (jnp.log(nf / max_exact) / math.log(T5_MAX_DISTANCE / max_exact)
                         * (half - max_exact)).astype(jnp.int32)
    large = jnp.minimum(large, half - 1)
    return ret + jnp.where(n < max_exact, n, large)


def _to_sub(t, dilation):
    b, s, h, dh = t.shape
    return t.reshape(b, s // dilation, dilation, h, dh).transpose(0, 2, 3, 1, 4)


def dilated_group_attention(q, k, v, bias_table, dilation, reach):
    b, s, h, dh = q.shape
    length = s // dilation
    nb = -(-length // reach)
    lp = nb * reach
    qs = jnp.pad(_to_sub(q, dilation), ((0, 0), (0, 0), (0, 0), (0, lp - length), (0, 0)))
    kv_pad = ((0, 0), (0, 0), (0, 0), (reach, lp - length + reach), (0, 0))
    ks = jnp.pad(_to_sub(k, dilation), kv_pad)
    vs = jnp.pad(_to_sub(v, dilation), kv_pad)
    rel = np.arange(3 * reach)[None, :] - reach - np.arange(reach)[:, None]
    near = jnp.asarray(np.abs(rel) <= reach)
    bias = bias_table[:, t5_bucket(jnp.asarray(rel * dilation, dtype=jnp.int32))].astype(jnp.float32)
    scale = dh ** -0.5

    def block(i):
        start = i * reach
        qb = lax.dynamic_slice_in_dim(qs, start, reach, axis=3)
        kb = lax.dynamic_slice_in_dim(ks, start, 3 * reach, axis=3)
        vb = lax.dynamic_slice_in_dim(vs, start, 3 * reach, axis=3)
        key_pos = start - reach + jnp.arange(3 * reach)
        valid = near & ((key_pos >= 0) & (key_pos < length))[None, :]
        logits = jnp.einsum('brhqd,brhkd->brhqk', qb, kb).astype(jnp.float32) * scale + bias
        logits = jnp.where(valid, logits, NEG_INF)
        mx = jnp.max(logits, axis=-1, keepdims=True)
        p = jnp.exp(logits - mx)
        denom = jnp.sum(p, axis=-1)
        out = jnp.einsum('brhqk,brhkd->brhqd', p.astype(vb.dtype), vb).astype(jnp.float32) / denom[..., None]
        return out, mx[..., 0] + jnp.log(denom)

    out, lse = lax.map(block, jnp.arange(nb))
    out = out.transpose(1, 2, 3, 0, 4, 5).reshape(b, dilation, h, lp, dh)[:, :, :, :length]
    out = out.transpose(0, 3, 1, 2, 4).reshape(b, s, h, dh)
    lse = lse.transpose(1, 2, 3, 0, 4).reshape(b, dilation, h, lp)[:, :, :, :length]
    lse = lse.transpose(0, 3, 1, 2).reshape(b, s, h)
    return out, lse


def mixer_a(hn, w_in, w_out, q_gain, k_gain, t5_bias):
    b, s, _ = hn.shape
    proj = hn @ w_in
    n_qkv = N_DIL_GROUPS * 3 * BRANCH_WIDTH
    qkv = proj[..., :n_qkv].reshape(b, s, N_DIL_GROUPS, 3, N_HEADS, HEAD_DIM)
    gate = proj[..., n_qkv:]
    outs, lses = [], []
    for g, (window, dilation) in enumerate(DILATED_PAIRS):
        q = rms_norm(qkv[:, :, g, 0], q_gain[g])
        k = rms_norm(qkv[:, :, g, 1], k_gain[g])
        v = qkv[:, :, g, 2]
        reach = (window // 2) // dilation
        o, lse = dilated_group_attention(q, k, v, t5_bias[g * N_HEADS:(g + 1) * N_HEADS],
                                         dilation, reach)
        outs.append(o)
        lses.append(lse)
    alpha = jax.nn.softmax(jnp.stack(lses), axis=0)
    y = jnp.sum(alpha[..., None] * jnp.stack(outs), axis=0)
    y = y.reshape(b, s, BRANCH_WIDTH).astype(hn.dtype) * jax.nn.silu(gate)
    return y @ w_out


def mixer_b(hn, w_in, w_out, q_gain, k_gain, rpb):
    b, s, _ = hn.shape
    rows = s // GRID_W
    wr = min(NA_ROWS, rows)
    proj = hn @ w_in
    q, k, v, gate = jnp.split(proj, 4, axis=-1)
    q = rms_norm(q.reshape(b, s, N_HEADS, HEAD_DIM), q_gain) * (HEAD_DIM ** -0.5)
    k = rms_norm(k.reshape(b, s, N_HEADS, HEAD_DIM), k_gain)
    v = v.reshape(b, s, N_HEADS, HEAD_DIM)

    def to_grid(t):
        return t.reshape(b, rows, GRID_W, N_HEADS, HEAD_DIM).transpose(0, 3, 1, 2, 4)

    qg, kg, vg = to_grid(q), to_grid(k), to_grid(v)
    n_cb = GRID_W // NA_COL_BLOCK
    qcol = np.arange(GRID_W).reshape(n_cb, NA_COL_BLOCK)
    cstart = np.clip(qcol - NA_COLS // 2, 0, GRID_W - NA_COLS)
    slab0 = np.clip(np.arange(n_cb) * NA_COL_BLOCK - NA_COLS // 2, 0, GRID_W - NA_SLAB)
    slab_cols = slab0[:, None] + np.arange(NA_SLAB)
    kc = slab_cols[:, None, :]
    col_valid = jnp.asarray((kc >= cstart[..., None]) & (kc < cstart[..., None] + NA_COLS))
    col_idx = np.clip(kc - qcol[..., None], -(NA_COLS - 1), NA_COLS - 1) + NA_COLS - 1
    col_bias = rpb[:, :, col_idx].astype(jnp.float32)
    slab_cols_j = jnp.asarray(slab_cols)

    def row(r):
        rs = jnp.clip(r - wr // 2, 0, rows - wr)
        qr = lax.dynamic_index_in_dim(qg, r, axis=2, keepdims=False)
        qr = qr.reshape(b, N_HEADS, n_cb, NA_COL_BLOCK, HEAD_DIM)
        kr = lax.dynamic_slice_in_dim(kg, rs, wr, axis=2)[:, :, :, slab_cols_j, :]
        vr = lax.dynamic_slice_in_dim(vg, rs, wr, axis=2)[:, :, :, slab_cols_j, :]
        row_idx = rs + jnp.arange(wr) - r + NA_ROWS - 1
        bias = col_bias[:, row_idx].transpose(0, 2, 3, 1, 4)
        logits = jnp.einsum('bhcqd,bhrcjd->bhcqrj', qr, kr).astype(jnp.float32) + bias
        logits = jnp.where(col_valid[:, :, None, :], logits, NEG_INF)
        p = jax.nn.softmax(logits.reshape(b, N_HEADS, n_cb, NA_COL_BLOCK, wr * NA_SLAB), axis=-1)
        p = p.reshape(b, N_HEADS, n_cb, NA_COL_BLOCK, wr, NA_SLAB).astype(vr.dtype)
        out = jnp.einsum('bhcqrj,bhrcjd->bhcqd', p, vr)
        return out.reshape(b, N_HEADS, GRID_W, HEAD_DIM)

    out = lax.map(row, jnp.arange(rows))
    y = out.transpose(1, 0, 3, 2, 4).reshape(b, s, BRANCH_WIDTH).astype(hn.dtype)
    y = y * jax.nn.silu(gate)
    return y @ w_out


def setup_inputs(seed: int = 0) -> dict:
    key = jax.random.key(seed)
    ks = jax.random.split(key, 13)
    f32 = jnp.float32
    x = jax.random.normal(ks[0], (BATCH, SEQ, D_MODEL), f32)
    norm_gain = 1.0 + 0.02 * jax.random.normal(ks[1], (DEPTH, D_MODEL), f32)
    a_w_in = jax.random.normal(ks[2], (N_A_LAYERS, D_MODEL, A_IN_COLS), f32) * D_MODEL ** -0.5
    a_w_out = jax.random.normal(ks[3], (N_A_LAYERS, BRANCH_WIDTH, D_MODEL), f32) * BRANCH_WIDTH ** -0.5
    a_q_gain = 1.0 + 0.02 * jax.random.normal(ks[4], (N_A_LAYERS, N_DIL_GROUPS, HEAD_DIM), f32)
    a_k_gain = 1.0 + 0.02 * jax.random.normal(ks[5], (N_A_LAYERS, N_DIL_GROUPS, HEAD_DIM), f32)
    t5_bias = 0.1 * jax.random.normal(ks[6], (N_DIL_GROUPS * N_HEADS, T5_BUCKETS), f32)
    b_w_in = jax.random.normal(ks[7], (N_B_LAYERS, D_MODEL, B_IN_COLS), f32) * D_MODEL ** -0.5
    b_w_out = jax.random.normal(ks[8], (N_B_LAYERS, BRANCH_WIDTH, D_MODEL), f32) * BRANCH_WIDTH ** -0.5
    b_q_gain = 1.0 + 0.02 * jax.random.normal(ks[9], (N_B_LAYERS, HEAD_DIM), f32)
    b_k_gain = 1.0 + 0.02 * jax.random.normal(ks[10], (N_B_LAYERS, HEAD_DIM), f32)
    b_rpb = 0.1 * jax.random.normal(ks[11], (N_B_LAYERS, N_HEADS, 2 * NA_ROWS - 1, 2 * NA_COLS - 1), f32)
    return {"x": x, "norm_gain": norm_gain, "a_w_in": a_w_in, "a_w_out": a_w_out,
            "a_q_gain": a_q_gain, "a_k_gain": a_k_gain, "t5_bias": t5_bias,
            "b_w_in": b_w_in, "b_w_out": b_w_out, "b_q_gain": b_q_gain,
            "b_k_gain": b_k_gain, "b_rpb": b_rpb}


def reference(x, norm_gain, a_w_in, a_w_out, a_q_gain, a_k_gain, t5_bias,
              b_w_in, b_w_out, b_q_gain, b_k_gain, b_rpb):
    for i in range(DEPTH):
        hn = rms_norm(x, norm_gain[i])
        j = i // N_MIXERS
        if i % N_MIXERS == 0:
            y = mixer_a(hn, a_w_in[j], a_w_out[j], a_q_gain[j], a_k_gain[j], t5_bias)
        else:
            y = mixer_b(hn, b_w_in[j], b_w_out[j], b_q_gain[j], b_k_gain[j], b_rpb[j])
        x = x + y.astype(x.dtype)
    return x
```

```python
import functools
import math

import numpy as np
import jax
import jax.numpy as jnp
from jax import lax
from jax.experimental import pallas as pl
from jax.experimental.pallas import tpu as pltpu

D_MODEL = 1024
HEAD_DIM = 64
N_HEADS = D_MODEL // HEAD_DIM
BRANCH = N_HEADS * HEAD_DIM
DILATED_PAIRS = ((128, 1), (512, 4), (2048, 16))
N_GROUPS = len(DILATED_PAIRS)
T5_BUCKETS = 32
T5_MAX_DISTANCE = 1024
GRID_W = 64
NA_ROWS = 8
NA_COLS = 16
RMS_EPS = 1e-6
MASK_VALUE = -1e30

LANES = 128
V7X_VMEM_BYTES = 64 * 2 ** 20
PAIR = LANES // HEAD_DIM

F32 = jnp.float32
BF16 = jnp.bfloat16


def _vmem_limit(block_bytes, extra_bytes):
  need = 2 * block_bytes + extra_bytes
  return int(min(need + need // 4, V7X_VMEM_BYTES - 8 * 2 ** 20))


def _in_proj_kernel(x_ref, ng_ref, w_ref, cg_ref, o_ref, hn_ref, *, tiles_per_branch, n_groups):
  j = pl.program_id(1)

  @pl.when(j == 0)
  def _():
    xf = x_ref[...]
    r = lax.rsqrt(jnp.mean(xf * xf, axis=-1, keepdims=True) + RMS_EPS)
    hn_ref[...] = ((xf * r) * ng_ref[...]).astype(BF16)

  acc = jnp.dot(hn_ref[...], w_ref[...], preferred_element_type=F32)
  which = j // tiles_per_branch
  is_norm = jnp.logical_and(which % 3 < 2, which < 3 * n_groups)

  @pl.when(is_norm)
  def _():
    tm, tn = acc.shape
    lo = lax.broadcasted_iota(jnp.int32, (tm, LANES), 1) < HEAD_DIM
    for c in range(tn // LANES):
      a = acc[:, c * LANES:(c + 1) * LANES]
      a2 = a * a
      s_lo = jnp.sum(jnp.where(lo, a2, 0.0), axis=-1, keepdims=True)
      s_hi = jnp.sum(jnp.where(lo, 0.0, a2), axis=-1, keepdims=True)
      r_lo = lax.rsqrt(s_lo * (1.0 / HEAD_DIM) + RMS_EPS)
      r_hi = lax.rsqrt(s_hi * (1.0 / HEAD_DIM) + RMS_EPS)
      y = (a * jnp.where(lo, r_lo, r_hi)) * cg_ref[:, c * LANES:(c + 1) * LANES]
      o_ref[:, c * LANES:(c + 1) * LANES] = y.astype(BF16)

  @pl.when(jnp.logical_not(is_norm))
  def _():
    o_ref[...] = acc.astype(BF16)


def _in_proj(x2d, norm_gain, w_bf16, col_gain, n_groups, tm=1024, tn=512):
  n, d = x2d.shape
  c = w_bf16.shape[1]
  assert n % tm == 0 and c % tn == 0 and BRANCH % tn == 0
  block_bytes = tm * d * 4 + d * tn * 2 + tm * tn * 2 + (d + tn) * 4
  kern = functools.partial(_in_proj_kernel, tiles_per_branch=BRANCH // tn, n_groups=n_groups)
  return pl.pallas_call(
      kern,
      grid=(n // tm, c // tn),
      in_specs=[
          pl.BlockSpec((tm, d), lambda i, j: (i, 0)),
          pl.BlockSpec((1, d), lambda i, j: (0, 0)),
          pl.BlockSpec((d, tn), lambda i, j: (0, j)),
          pl.BlockSpec((1, tn), lambda i, j: (0, j)),
      ],
      out_specs=pl.BlockSpec((tm, tn), lambda i, j: (i, j)),
      out_shape=jax.ShapeDtypeStruct((n, c), BF16),
      scratch_shapes=[pltpu.VMEM((tm, d), BF16)],
      compiler_params=pltpu.CompilerParams(
          dimension_semantics=("parallel", "arbitrary"),
          vmem_limit_bytes=_vmem_limit(block_bytes, tm * d * 2 + 4 * tm * tn * 4)),
      name="in_proj",
  )(x2d, norm_gain.reshape(1, d), w_bf16, col_gain.reshape(1, c))


def _pair_attention(q, k, v, bias, want_lse):
  tq = q.shape[0]
  tk = k.shape[0]
  lo = lax.broadcasted_iota(jnp.int32, (tq, LANES), 1) < HEAD_DIM
  zero = jnp.zeros_like(q)
  q2 = jnp.concatenate([jnp.where(lo, q, zero), jnp.where(lo, zero, q)], axis=0)
  s = lax.dot_general(q2, k, (((1,), (1,)), ((), ())), preferred_element_type=F32)
  s = s + bias.reshape(2 * tq, tk)
  m = jnp.max(s, axis=-1, keepdims=True)
  p = jnp.exp(s - m)
  l = jnp.sum(p, axis=-1, keepdims=True)
  pv = jnp.dot(p.astype(BF16), v, preferred_element_type=F32)
  inv = 1.0 / l
  out = jnp.where(lo, pv[:tq] * inv[:tq], pv[tq:] * inv[tq:])
  if not want_lse:
    return out, None
  lse = m + jnp.log(l)
  return out, jnp.where(lo, lse[:tq], lse[tq:])


A_TQ = 128
A_TK = 256


def _dilated_attn_kernel(q_ref, k_ref, v_ref, b_ref, o_ref, l_ref, *, n_pairs, seq):
  n_tiles = seq // A_TQ

  for p in range(n_pairs):
    lanes = slice(p * LANES, (p + 1) * LANES)

    def tile(t, carry, lanes=lanes, p=p):
      q0 = pl.multiple_of(t * A_TQ, A_TQ)
      ws = pl.multiple_of(jnp.clip(q0 - A_TQ // 2, 0, seq - A_TK), A_TQ // 2)
      var = jnp.where(t == 0, 0, jnp.where(t == n_tiles - 1, 2, 1))
      q = q_ref[0, pl.ds(q0, A_TQ), lanes]
      k = k_ref[0, pl.ds(ws, A_TK), lanes]
      v = v_ref[0, pl.ds(ws, A_TK), lanes]
      bias = b_ref[var, PAIR * p:PAIR * (p + 1)]
      out, lse = _pair_attention(q, k, v, bias, True)
      o_ref[0, pl.ds(q0, A_TQ), lanes] = out.astype(BF16)
      l_ref[0, pl.ds(q0, A_TQ), lanes] = lse
      return carry

    lax.fori_loop(0, n_tiles, tile, 0)


def _t5_bucket(rel):
  half = T5_BUCKETS // 2
  max_exact = half // 2
  ret = jnp.where(rel > 0, half, 0)
  n = jnp.abs(rel)
  nf = jnp.maximum(n, 1).astype(jnp.float32)
  large = max_exact + (jnp.log(nf / max_exact) / math.log(T5_MAX_DISTANCE / max_exact)
                       * (half - max_exact)).astype(jnp.int32)
  large = jnp.minimum(large, half - 1)
  return ret + jnp.where(n < max_exact, n, large)


def _dilated_bias(table, dilation, reach):
  a = np.arange(A_TQ)[:, None]
  jj = np.arange(A_TK)[None, :]
  tiles = []
  for delta in (0, -reach, -2 * reach):
    rel = delta + jj - a
    bias = table[:, _t5_bucket(jnp.asarray(rel * dilation, dtype=jnp.int32))].astype(F32)
    tiles.append(jnp.where(jnp.asarray(np.abs(rel) <= reach), bias, MASK_VALUE))
  return jnp.stack(tiles)


def _dilated_attention(proj, group, dilation, bias, n_pairs):
  b, s, c = proj.shape
  seq = s // dilation
  w = LANES * n_pairs
  n_hb = BRANCH // w
  pv = proj.reshape(b, seq, dilation * c)
  col0 = group * 3 * n_hb

  def in_map(t):
    return lambda hb, bi, r: (bi, 0, r * (c // w) + col0 + t * n_hb + hb)

  out_map = lambda hb, bi, r: (bi, 0, r * n_hb + hb)
  blk = (1, seq, w)
  block_bytes = seq * w * (3 * 2 + 2 + 4) + 3 * PAIR * n_pairs * A_TQ * A_TK * 4
  kern = functools.partial(_dilated_attn_kernel, n_pairs=n_pairs, seq=seq)
  o, lse = pl.pallas_call(
      kern,
      grid=(n_hb, b, dilation),
      in_specs=[
          pl.BlockSpec(blk, in_map(0)),
          pl.BlockSpec(blk, in_map(1)),
          pl.BlockSpec(blk, in_map(2)),
          pl.BlockSpec((3, PAIR * n_pairs, A_TQ, A_TK), lambda hb, bi, r: (0, hb, 0, 0)),
      ],
      out_specs=[pl.BlockSpec(blk, out_map), pl.BlockSpec(blk, out_map)],
      out_shape=[jax.ShapeDtypeStruct((b, seq, dilation * BRANCH), BF16),
                 jax.ShapeDtypeStruct((b, seq, dilation * BRANCH), F32)],
      compiler_params=pltpu.CompilerParams(
          dimension_semantics=("parallel", "parallel", "parallel"),
          vmem_limit_bytes=_vmem_limit(block_bytes, 8 * 2 ** 20)),
      name=f"dilated_attn_g{group}",
  )(pv, pv, pv, bias)
  return o.reshape(b, s, BRANCH), lse.reshape(b, s, BRANCH)


def _na_attn_kernel(q_ref, k_ref, v_ref, b_ref, o_ref, *, rows):
  tk = NA_ROWS * GRID_W

  def row(r, carry):
    rs = jnp.clip(r - NA_ROWS // 2, 0, rows - NA_ROWS)
    q0 = pl.multiple_of(r * GRID_W, GRID_W)
    ws = pl.multiple_of(rs * GRID_W, GRID_W)
    q = q_ref[0, pl.ds(q0, GRID_W), :]
    k = k_ref[0, pl.ds(ws, tk), :]
    v = v_ref[0, pl.ds(ws, tk), :]
    bias = b_ref[rs - r + NA_ROWS - 1]
    out, _ = _pair_attention(q, k, v, bias, False)
    o_ref[0, pl.ds(q0, GRID_W), :] = out.astype(BF16)
    return carry

  lax.fori_loop(0, rows, row, 0)


def _na_bias(rpb):
  qc = np.arange(GRID_W)[:, None]
  kc = np.arange(GRID_W)[None, :]
  cstart = np.clip(qc - NA_COLS // 2, 0, GRID_W - NA_COLS)
  valid = jnp.asarray((kc >= cstart) & (kc < cstart + NA_COLS))
  col_idx = np.clip(kc - qc, -(NA_COLS - 1), NA_COLS - 1) + NA_COLS - 1
  col_bias = jnp.where(valid[None, None], rpb[:, :, col_idx].astype(F32), MASK_VALUE)
  variants = []
  for u in range(NA_ROWS):
    rows_u = col_bias[:, u:u + NA_ROWS]
    variants.append(rows_u.transpose(0, 2, 1, 3).reshape(N_HEADS, GRID_W, NA_ROWS * GRID_W))
  return jnp.stack(variants)


def _na_attention(proj, bias):
  b, s, c = proj.shape
  rows = s // GRID_W
  n_hb = BRANCH // LANES
  blk = (1, s, LANES)
  tk = NA_ROWS * GRID_W
  block_bytes = s * LANES * 2 * 4 + NA_ROWS * PAIR * GRID_W * tk * 4
  return pl.pallas_call(
      functools.partial(_na_attn_kernel, rows=rows),
      grid=(n_hb, b),
      in_specs=[
          pl.BlockSpec(blk, lambda hb, bi: (bi, 0, hb)),
          pl.BlockSpec(blk, lambda hb, bi: (bi, 0, n_hb + hb)),
          pl.BlockSpec(blk, lambda hb, bi: (bi, 0, 2 * n_hb + hb)),
          pl.BlockSpec((NA_ROWS, PAIR, GRID_W, tk), lambda hb, bi: (0, hb, 0, 0)),
      ],
      out_specs=pl.BlockSpec(blk, lambda hb, bi: (bi, 0, hb)),
      out_shape=jax.ShapeDtypeStruct((b, s, BRANCH), BF16),
      compiler_params=pltpu.CompilerParams(
          dimension_semantics=("parallel", "parallel"),
          vmem_limit_bytes=_vmem_limit(block_bytes, 8 * 2 ** 20)),
      name="na_attn",
  )(proj, proj, proj, bias)


def _out_proj_kernel(*refs, n_groups):
  o_refs = refs[:n_groups]
  l_refs = refs[n_groups:2 * n_groups] if n_groups > 1 else ()
  g_ref, x_ref, w_ref, out_ref = refs[-4:]
  if n_groups == 1:
    y = o_refs[0][...].astype(F32)
  else:
    lses = [r[...] for r in l_refs]
    mx = functools.reduce(jnp.maximum, lses)
    es = [jnp.exp(l - mx) for l in lses]
    den = functools.reduce(lambda a, c: a + c, es)
    num = functools.reduce(lambda a, c: a + c,
                           [e * r[...].astype(F32) for e, r in zip(es, o_refs)])
    y = num / den
  y = y * jax.nn.silu(g_ref[...].astype(F32))
  out_ref[...] = x_ref[...] + jnp.dot(y.astype(BF16), w_ref[...], preferred_element_type=F32)


def _out_proj(outs, lses, proj2d, gate_block, x2d, w_bf16, tm=256):
  n, d = x2d.shape
  n_groups = len(outs)
  row = lambda i: (i, 0)
  act = pl.BlockSpec((tm, BRANCH), row)
  block_bytes = tm * BRANCH * (2 * n_groups + 4 * len(lses) + 2) + 2 * tm * d * 4 + BRANCH * d * 2
  return pl.pallas_call(
      functools.partial(_out_proj_kernel, n_groups=n_groups),
      grid=(n // tm,),
      in_specs=[act] * (n_groups + len(lses)) + [
          pl.BlockSpec((tm, BRANCH), lambda i: (i, gate_block)),
          pl.BlockSpec((tm, d), row),
          pl.BlockSpec((BRANCH, d), lambda i: (0, 0)),
      ],
      out_specs=pl.BlockSpec((tm, d), row),
      out_shape=jax.ShapeDtypeStruct((n, d), F32),
      compiler_params=pltpu.CompilerParams(
          dimension_semantics=("parallel",),
          vmem_limit_bytes=_vmem_limit(block_bytes, 8 * tm * BRANCH * 4)),
      name=f"out_proj_g{n_groups}",
  )(*outs, *lses, proj2d, x2d, w_bf16)


def _col_gain(q_gain, k_gain):
  q = jnp.tile(q_gain.astype(F32) * HEAD_DIM ** -0.5, N_HEADS)
  k = jnp.tile(k_gain.astype(F32), N_HEADS)
  return jnp.concatenate([q, k, jnp.ones((BRANCH,), F32)])


def _layer_a(x, norm_gain, w_in, w_out, q_gain, k_gain, t5_bias):
  b, s, d = x.shape
  x2d = x.reshape(b * s, d)
  gains = [_col_gain(q_gain[g], k_gain[g]) for g in range(N_GROUPS)] + [jnp.ones((BRANCH,), F32)]
  proj = _in_proj(x2d, norm_gain, w_in.astype(BF16), jnp.concatenate(gains), N_GROUPS)
  proj3 = proj.reshape(b, s, -1)
  outs, lses = [], []
  for g, (window, dilation) in enumerate(DILATED_PAIRS):
    reach = (window // 2) // dilation
    assert 2 * reach == A_TQ and s // dilation >= A_TK
    bias = _dilated_bias(t5_bias[g * N_HEADS:(g + 1) * N_HEADS], dilation, reach)
    n_pairs = max(1, min(N_HEADS // PAIR, (4096 // (s // dilation))))
    o, lse = _dilated_attention(proj3, g, dilation, bias, n_pairs)
    outs.append(o.reshape(b * s, BRANCH))
    lses.append(lse.reshape(b * s, BRANCH))
  y = _out_proj(outs, lses, proj, 3 * N_GROUPS, x2d, w_out.astype(BF16))
  return y.reshape(b, s, d)


def _layer_b(x, norm_gain, w_in, w_out, q_gain, k_gain, rpb):
  b, s, d = x.shape
  x2d = x.reshape(b * s, d)
  gains = jnp.concatenate([_col_gain(q_gain, k_gain), jnp.ones((BRANCH,), F32)])
  proj = _in_proj(x2d, norm_gain, w_in.astype(BF16), gains, 1)
  o = _na_attention(proj.reshape(b, s, -1), _na_bias(rpb))
  y = _out_proj([o.reshape(b * s, BRANCH)], [], proj, 3, x2d, w_out.astype(BF16))
  return y.reshape(b, s, d)


def kernel(x, norm_gain, a_w_in, a_w_out, a_q_gain, a_k_gain, t5_bias,
           b_w_in, b_w_out, b_q_gain, b_k_gain, b_rpb):
  depth = norm_gain.shape[0]
  for i in range(depth):
    j = i // 2
    if i % 2 == 0:
      x = _layer_a(x, norm_gain[i], a_w_in[j], a_w_out[j], a_q_gain[j], a_k_gain[j], t5_bias)
    else:
      x = _layer_b(x, norm_gain[i], b_w_in[j], b_w_out[j], b_q_gain[j], b_k_gain[j], b_rpb[j])
  return x
```

```python
import functools
import math

import numpy as np
import jax
import jax.numpy as jnp
from jax import lax
from jax.experimental import pallas as pl
from jax.experimental.pallas import tpu as pltpu

D_MODEL = 1024
HEAD_DIM = 64
N_HEADS = D_MODEL // HEAD_DIM
BRANCH = N_HEADS * HEAD_DIM
DILATED_PAIRS = ((128, 1), (512, 4), (2048, 16))
N_GROUPS = len(DILATED_PAIRS)
T5_BUCKETS = 32
T5_MAX_DISTANCE = 1024
GRID_W = 64
NA_ROWS = 8
NA_COLS = 16
RMS_EPS = 1e-6
MASK_VALUE = -1e30
LOG2E = math.log2(math.e)
LN2 = math.log(2.0)

LANES = 128
V7X_VMEM_BYTES = 64 * 2 ** 20
PAIR = LANES // HEAD_DIM

F32 = jnp.float32
BF16 = jnp.bfloat16


def _vmem_limit(block_bytes, extra_bytes):
  need = 2 * block_bytes + extra_bytes
  return int(min(need + need // 4, V7X_VMEM_BYTES - 8 * 2 ** 20))


def _in_proj_kernel(x_ref, ng_ref, w_ref, cg_ref, o_ref, hn_ref, *, tiles_per_branch, n_groups):
  j = pl.program_id(1)

  @pl.when(j == 0)
  def _():
    xf = x_ref[...]
    r = lax.rsqrt(jnp.mean(xf * xf, axis=-1, keepdims=True) + RMS_EPS)
    hn_ref[...] = ((xf * r) * ng_ref[...]).astype(BF16)

  acc = jnp.dot(hn_ref[...], w_ref[...], preferred_element_type=F32)
  which = j // tiles_per_branch
  is_norm = jnp.logical_and(which % 3 < 2, which < 3 * n_groups)

  @pl.when(is_norm)
  def _():
    tm, tn = acc.shape
    lo = lax.broadcasted_iota(jnp.int32, (tm, LANES), 1) < HEAD_DIM
    for c in range(tn // LANES):
      a = acc[:, c * LANES:(c + 1) * LANES]
      a2 = a * a
      s_lo = jnp.sum(jnp.where(lo, a2, 0.0), axis=-1, keepdims=True)
      s_hi = jnp.sum(jnp.where(lo, 0.0, a2), axis=-1, keepdims=True)
      r_lo = lax.rsqrt(s_lo * (1.0 / HEAD_DIM) + RMS_EPS)
      r_hi = lax.rsqrt(s_hi * (1.0 / HEAD_DIM) + RMS_EPS)
      y = (a * jnp.where(lo, r_lo, r_hi)) * cg_ref[:, c * LANES:(c + 1) * LANES]
      o_ref[:, c * LANES:(c + 1) * LANES] = y.astype(BF16)

  @pl.when(jnp.logical_not(is_norm))
  def _():
    o_ref[...] = acc.astype(BF16)


def _in_proj(x2d, norm_gain, w_bf16, col_gain, n_groups, tm=1024, tn=512):
  n, d = x2d.shape
  c = w_bf16.shape[1]
  assert n % tm == 0 and c % tn == 0 and BRANCH % tn == 0
  block_bytes = tm * d * 4 + d * tn * 2 + tm * tn * 2 + (d + tn) * 4
  kern = functools.partial(_in_proj_kernel, tiles_per_branch=BRANCH // tn, n_groups=n_groups)
  return pl.pallas_call(
      kern,
      grid=(n // tm, c // tn),
      in_specs=[
          pl.BlockSpec((tm, d), lambda i, j: (i, 0)),
          pl.BlockSpec((1, d), lambda i, j: (0, 0)),
          pl.BlockSpec((d, tn), lambda i, j: (0, j)),
          pl.BlockSpec((1, tn), lambda i, j: (0, j)),
      ],
      out_specs=pl.BlockSpec((tm, tn), lambda i, j: (i, j)),
      out_shape=jax.ShapeDtypeStruct((n, c), BF16),
      scratch_shapes=[pltpu.VMEM((tm, d), BF16)],
      compiler_params=pltpu.CompilerParams(
          dimension_semantics=("parallel", "arbitrary"),
          vmem_limit_bytes=_vmem_limit(block_bytes, tm * d * 2 + 4 * tm * tn * 4)),
      name="in_proj",
  )(x2d, norm_gain.reshape(1, d), w_bf16, col_gain.reshape(1, c))


def _pair_attention(q, k, v, bias, want_lse):
  tq = q.shape[0]
  tk = k.shape[0]
  lo = lax.broadcasted_iota(jnp.int32, (tq, LANES), 1) < HEAD_DIM
  zero = jnp.zeros_like(q)
  q2 = jnp.concatenate([jnp.where(lo, q, zero), jnp.where(lo, zero, q)], axis=0)
  s = lax.dot_general(q2, k, (((1,), (1,)), ((), ())), preferred_element_type=F32)
  s = s + bias.reshape(2 * tq, tk)
  m = jnp.max(s, axis=-1, keepdims=True)
  p = jnp.exp2(s - m)
  l = jnp.sum(p, axis=-1, keepdims=True)
  pv = jnp.dot(p.astype(BF16), v, preferred_element_type=F32)
  inv = 1.0 / l
  out = jnp.where(lo, pv[:tq] * inv[:tq], pv[tq:] * inv[tq:])
  if not want_lse:
    return out, None
  lse = (m + jnp.log2(l)) * LN2
  return out, jnp.where(lo, lse[:tq], lse[tq:])


A_TQ = 128
A_TK = 256


def _dilated_attn_kernel(q_ref, k_ref, v_ref, b_ref, o_ref, l_ref, *, n_pairs, seq):
  n_tiles = seq // A_TQ

  for p in range(n_pairs):
    lanes = slice(p * LANES, (p + 1) * LANES)

    def tile(t, carry, lanes=lanes, p=p):
      q0 = pl.multiple_of(t * A_TQ, A_TQ)
      ws = pl.multiple_of(jnp.clip(q0 - A_TQ // 2, 0, seq - A_TK), A_TQ // 2)
      var = jnp.where(t == 0, 0, jnp.where(t == n_tiles - 1, 2, 1))
      q = q_ref[0, pl.ds(q0, A_TQ), lanes]
      k = k_ref[0, pl.ds(ws, A_TK), lanes]
      v = v_ref[0, pl.ds(ws, A_TK), lanes]
      bias = b_ref[var, PAIR * p:PAIR * (p + 1)]
      out, lse = _pair_attention(q, k, v, bias, True)
      o_ref[0, pl.ds(q0, A_TQ), lanes] = out.astype(BF16)
      l_ref[0, pl.ds(q0, A_TQ), lanes] = lse
      return carry

    lax.fori_loop(0, n_tiles, tile, 0, unroll=min(4, n_tiles))


def _t5_bucket(rel):
  half = T5_BUCKETS // 2
  max_exact = half // 2
  ret = jnp.where(rel > 0, half, 0)
  n = jnp.abs(rel)
  nf = jnp.maximum(n, 1).astype(jnp.float32)
  large = max_exact + (jnp.log(nf / max_exact) / math.log(T5_MAX_DISTANCE / max_exact)
                       * (half - max_exact)).astype(jnp.int32)
  large = jnp.minimum(large, half - 1)
  return ret + jnp.where(n < max_exact, n, large)


def _lookup(table, idx, n):
  onehot = (idx[..., None] == jnp.arange(n, dtype=idx.dtype)).astype(F32)
  out = jnp.einsum("...n,pn->p...", onehot, table.reshape(-1, n), precision=lax.Precision.HIGHEST)
  return out.reshape(table.shape[:-1] + idx.shape)


def _dilated_bias(table, dilation, reach):
  a = np.arange(A_TQ)[:, None]
  jj = np.arange(A_TK)[None, :]
  tiles = []
  for delta in (0, -reach, -2 * reach):
    rel = delta + jj - a
    bucket = _t5_bucket(jnp.asarray(rel * dilation, dtype=jnp.int32))
    bias = _lookup(table.astype(F32), bucket, T5_BUCKETS)
    tiles.append(jnp.where(jnp.asarray(np.abs(rel) <= reach), bias * LOG2E, MASK_VALUE))
  return jnp.stack(tiles)


def _dilated_attention(proj, group, dilation, bias, n_pairs):
  b, s, c = proj.shape
  seq = s // dilation
  w = LANES * n_pairs
  n_hb = BRANCH // w
  pv = proj.reshape(b, seq, dilation * c)
  col0 = group * 3 * n_hb

  def in_map(t):
    return lambda hb, bi, r: (bi, 0, r * (c // w) + col0 + t * n_hb + hb)

  out_map = lambda hb, bi, r: (bi, 0, r * n_hb + hb)
  blk = (1, seq, w)
  block_bytes = seq * w * (3 * 2 + 2 + 4) + 3 * PAIR * n_pairs * A_TQ * A_TK * 4
  kern = functools.partial(_dilated_attn_kernel, n_pairs=n_pairs, seq=seq)
  o, lse = pl.pallas_call(
      kern,
      grid=(n_hb, b, dilation),
      in_specs=[
          pl.BlockSpec(blk, in_map(0)),
          pl.BlockSpec(blk, in_map(1)),
          pl.BlockSpec(blk, in_map(2)),
          pl.BlockSpec((3, PAIR * n_pairs, A_TQ, A_TK), lambda hb, bi, r: (0, hb, 0, 0)),
      ],
      out_specs=[pl.BlockSpec(blk, out_map), pl.BlockSpec(blk, out_map)],
      out_shape=[jax.ShapeDtypeStruct((b, seq, dilation * BRANCH), BF16),
                 jax.ShapeDtypeStruct((b, seq, dilation * BRANCH), F32)],
      compiler_params=pltpu.CompilerParams(
          dimension_semantics=("parallel", "parallel", "parallel"),
          vmem_limit_bytes=_vmem_limit(block_bytes, 8 * 2 ** 20)),
      name=f"dilated_attn_g{group}",
  )(pv, pv, pv, bias)
  return o.reshape(b, s, BRANCH), lse.reshape(b, s, BRANCH)


def _na_attn_kernel(q_ref, k_ref, v_ref, b_ref, o_ref, *, rows):
  tk = NA_ROWS * GRID_W

  def row(r, carry):
    rs = jnp.clip(r - NA_ROWS // 2, 0, rows - NA_ROWS)
    q0 = pl.multiple_of(r * GRID_W, GRID_W)
    ws = pl.multiple_of(rs * GRID_W, GRID_W)
    q = q_ref[0, pl.ds(q0, GRID_W), :]
    k = k_ref[0, pl.ds(ws, tk), :]
    v = v_ref[0, pl.ds(ws, tk), :]
    bias = b_ref[rs - r + NA_ROWS - 1]
    out, _ = _pair_attention(q, k, v, bias, False)
    o_ref[0, pl.ds(q0, GRID_W), :] = out.astype(BF16)
    return carry

  lax.fori_loop(0, rows, row, 0, unroll=4)


def _na_bias(rpb):
  qc = np.arange(GRID_W)[:, None]
  kc = np.arange(GRID_W)[None, :]
  cstart = np.clip(qc - NA_COLS // 2, 0, GRID_W - NA_COLS)
  valid = jnp.asarray((kc >= cstart) & (kc < cstart + NA_COLS))
  col_idx = np.clip(kc - qc, -(NA_COLS - 1), NA_COLS - 1) + NA_COLS - 1
  col_bias = _lookup(rpb.astype(F32), jnp.asarray(col_idx, dtype=jnp.int32), 2 * NA_COLS - 1)
  col_bias = jnp.where(valid[None, None], col_bias * LOG2E, MASK_VALUE)
  variants = []
  for u in range(NA_ROWS):
    rows_u = col_bias[:, u:u + NA_ROWS]
    variants.append(rows_u.transpose(0, 2, 1, 3).reshape(N_HEADS, GRID_W, NA_ROWS * GRID_W))
  return jnp.stack(variants)


def _na_attention(proj, bias):
  b, s, c = proj.shape
  rows = s // GRID_W
  n_hb = BRANCH // LANES
  blk = (1, s, LANES)
  tk = NA_ROWS * GRID_W
  block_bytes = s * LANES * 2 * 4 + NA_ROWS * PAIR * GRID_W * tk * 4
  return pl.pallas_call(
      functools.partial(_na_attn_kernel, rows=rows),
      grid=(n_hb, b),
      in_specs=[
          pl.BlockSpec(blk, lambda hb, bi: (bi, 0, hb)),
          pl.BlockSpec(blk, lambda hb, bi: (bi, 0, n_hb + hb)),
          pl.BlockSpec(blk, lambda hb, bi: (bi, 0, 2 * n_hb + hb)),
          pl.BlockSpec((NA_ROWS, PAIR, GRID_W, tk), lambda hb, bi: (0, hb, 0, 0)),
      ],
      out_specs=pl.BlockSpec(blk, lambda hb, bi: (bi, 0, hb)),
      out_shape=jax.ShapeDtypeStruct((b, s, BRANCH), BF16),
      compiler_params=pltpu.CompilerParams(
          dimension_semantics=("parallel", "parallel"),
          vmem_limit_bytes=_vmem_limit(block_bytes, 8 * 2 ** 20)),
      name="na_attn",
  )(proj, proj, proj, bias)


def _out_proj_kernel(*refs, n_groups):
  o_refs = refs[:n_groups]
  l_refs = refs[n_groups:2 * n_groups] if n_groups > 1 else ()
  g_ref, x_ref, w_ref, out_ref = refs[-4:]
  if n_groups == 1:
    y = o_refs[0][...].astype(F32)
  else:
    lses = [r[...] for r in l_refs]
    mx = functools.reduce(jnp.maximum, lses)
    es = [jnp.exp(l - mx) for l in lses]
    den = functools.reduce(lambda a, c: a + c, es)
    num = functools.reduce(lambda a, c: a + c,
                           [e * r[...].astype(F32) for e, r in zip(es, o_refs)])
    y = num / den
  y = y * jax.nn.silu(g_ref[...].astype(F32))
  out_ref[...] = x_ref[...] + jnp.dot(y.astype(BF16), w_ref[...], preferred_element_type=F32)


def _out_proj(outs, lses, proj2d, gate_block, x2d, w_bf16, tm=256):
  n, d = x2d.shape
  n_groups = len(outs)
  row = lambda i: (i, 0)
  act = pl.BlockSpec((tm, BRANCH), row)
  block_bytes = tm * BRANCH * (2 * n_groups + 4 * len(lses) + 2) + 2 * tm * d * 4 + BRANCH * d * 2
  return pl.pallas_call(
      functools.partial(_out_proj_kernel, n_groups=n_groups),
      grid=(n // tm,),
      in_specs=[act] * (n_groups + len(lses)) + [
          pl.BlockSpec((tm, BRANCH), lambda i: (i, gate_block)),
          pl.BlockSpec((tm, d), row),
          pl.BlockSpec((BRANCH, d), lambda i: (0, 0)),
      ],
      out_specs=pl.BlockSpec((tm, d), row),
      out_shape=jax.ShapeDtypeStruct((n, d), F32),
      compiler_params=pltpu.CompilerParams(
          dimension_semantics=("parallel",),
          vmem_limit_bytes=_vmem_limit(block_bytes, 8 * tm * BRANCH * 4)),
      name=f"out_proj_g{n_groups}",
  )(*outs, *lses, proj2d, x2d, w_bf16)


def _col_gain(q_gain, k_gain):
  q = jnp.tile(q_gain.astype(F32) * (HEAD_DIM ** -0.5 * LOG2E), N_HEADS)
  k = jnp.tile(k_gain.astype(F32), N_HEADS)
  return jnp.concatenate([q, k, jnp.ones((BRANCH,), F32)])


def _layer_a(x, norm_gain, w_in, w_out, q_gain, k_gain, t5_bias):
  b, s, d = x.shape
  x2d = x.reshape(b * s, d)
  gains = [_col_gain(q_gain[g], k_gain[g]) for g in range(N_GROUPS)] + [jnp.ones((BRANCH,), F32)]
  proj = _in_proj(x2d, norm_gain, w_in.astype(BF16), jnp.concatenate(gains), N_GROUPS)
  proj3 = proj.reshape(b, s, -1)
  outs, lses = [], []
  for g, (window, dilation) in enumerate(DILATED_PAIRS):
    reach = (window // 2) // dilation
    assert 2 * reach == A_TQ and s // dilation >= A_TK
    bias = _dilated_bias(t5_bias[g * N_HEADS:(g + 1) * N_HEADS], dilation, reach)
    n_pairs = max(1, min(N_HEADS // PAIR, (4096 // (s // dilation))))
    o, lse = _dilated_attention(proj3, g, dilation, bias, n_pairs)
    outs.append(o.reshape(b * s, BRANCH))
    lses.append(lse.reshape(b * s, BRANCH))
  y = _out_proj(outs, lses, proj, 3 * N_GROUPS, x2d, w_out.astype(BF16))
  return y.reshape(b, s, d)


def _layer_b(x, norm_gain, w_in, w_out, q_gain, k_gain, rpb):
  b, s, d = x.shape
  x2d = x.reshape(b * s, d)
  gains = jnp.concatenate([_col_gain(q_gain, k_gain), jnp.ones((BRANCH,), F32)])
  proj = _in_proj(x2d, norm_gain, w_in.astype(BF16), gains, 1)
  o = _na_attention(proj.reshape(b, s, -1), _na_bias(rpb))
  y = _out_proj([o.reshape(b * s, BRANCH)], [], proj, 3, x2d, w_out.astype(BF16))
  return y.reshape(b, s, d)


def kernel(x, norm_gain, a_w_in, a_w_out, a_q_gain, a_k_gain, t5_bias,
           b_w_in, b_w_out, b_q_gain, b_k_gain, b_rpb):
  depth = norm_gain.shape[0]
  for i in range(depth):
    j = i // 2
    if i % 2 == 0:
      x = _layer_a(x, norm_gain[i], a_w_in[j], a_w_out[j], a_q_gain[j], a_k_gain[j], t5_bias)
    else:
      x = _layer_b(x, norm_gain[i], b_w_in[j], b_w_out[j], b_q_gain[j], b_k_gain[j], b_rpb[j])
  return x
```

```python
import functools
import math

import numpy as np
import jax
import jax.numpy as jnp
from jax import lax
from jax.experimental import pallas as pl
from jax.experimental.pallas import tpu as pltpu

D_MODEL = 1024
HEAD_DIM = 64
N_HEADS = D_MODEL // HEAD_DIM
BRANCH = N_HEADS * HEAD_DIM
DILATED_PAIRS = ((128, 1), (512, 4), (2048, 16))
N_GROUPS = len(DILATED_PAIRS)
MAX_DILATION = max(d for _, d in DILATED_PAIRS)
T5_BUCKETS = 32
T5_MAX_DISTANCE = 1024
GRID_W = 64
NA_ROWS = 8
NA_COLS = 16
RMS_EPS = 1e-6
MASK_VALUE = -1e30
LOG2E = math.log2(math.e)
LN2 = math.log(2.0)

LANES = 128
BF16_ROWS = 16
V7X_VMEM_BYTES = 64 * 2 ** 20
PAIR = LANES // HEAD_DIM
N_SLABS = D_MODEL // LANES

F32 = jnp.float32
BF16 = jnp.bfloat16


def _vmem_limit(block_bytes, extra_bytes):
  need = 2 * block_bytes + extra_bytes
  return int(min(need + need // 4, V7X_VMEM_BYTES - 8 * 2 ** 20))


def _slab(c):
  return slice(c * LANES, (c + 1) * LANES)


def _head_norm(a, lo, gain):
  a2 = a * a
  s_lo = jnp.sum(jnp.where(lo, a2, 0.0), axis=-1, keepdims=True)
  s_hi = jnp.sum(jnp.where(lo, 0.0, a2), axis=-1, keepdims=True)
  r_lo = lax.rsqrt(s_lo * (1.0 / HEAD_DIM) + RMS_EPS)
  r_hi = lax.rsqrt(s_hi * (1.0 / HEAD_DIM) + RMS_EPS)
  return (a * jnp.where(lo, r_lo, r_hi)) * gain


def _in_proj_kernel(*refs, dilation, norm_roles):
  x_refs = refs[:N_SLABS]
  ng_ref, w_ref, cg_ref, o_ref, hn_ref = refs[N_SLABS:]
  tm = x_refs[0].shape[0]
  seg = tm // dilation

  for r in range(dilation):
    if dilation == 1:
      xs = [ref[...] for ref in x_refs]
    else:
      xs = [ref[pl.ds(r, seg, stride=dilation), :] for ref in x_refs]
    sq = functools.reduce(lambda a, b: a + b, [x * x for x in xs])
    inv = lax.rsqrt(jnp.sum(sq, axis=-1, keepdims=True) * (1.0 / D_MODEL) + RMS_EPS)
    for c in range(N_SLABS):
      hn_ref[r * seg:(r + 1) * seg, _slab(c)] = ((xs[c] * inv) * ng_ref[:, _slab(c)]).astype(BF16)

  lo = lax.broadcasted_iota(jnp.int32, (tm, LANES), 1) < HEAD_DIM
  for cb, is_norm in enumerate(norm_roles):
    acc = jnp.dot(hn_ref[...], w_ref[:, cb * BRANCH:(cb + 1) * BRANCH],
                  preferred_element_type=F32)
    for c in range(BRANCH // LANES):
      col = slice(cb * BRANCH + c * LANES, cb * BRANCH + (c + 1) * LANES)
      a = acc[:, _slab(c)]
      y = (_head_norm(a, lo, cg_ref[:, col]) if is_norm else a).astype(BF16)
      for r in range(dilation):
        o_ref[r, :, col] = y[r * seg:(r + 1) * seg]


def _in_proj(x2d, batch, norm_gain, w_bf16, col_gain, norm_roles, dilation, tm=512):
  n, d = x2d.shape
  c = w_bf16.shape[1]
  s = n // batch
  tiles = s // tm
  seg = tm // dilation
  assert s % tm == 0 and seg % BF16_ROWS == 0 and c == BRANCH * len(norm_roles)
  const = lambda i: (0, 0)
  block_bytes = tm * d * 4 + tm * c * 2 + (d + c) * 4
  kern = functools.partial(_in_proj_kernel, dilation=dilation, norm_roles=norm_roles)
  return pl.pallas_call(
      kern,
      grid=(n // tm,),
      in_specs=[pl.BlockSpec((tm, LANES), lambda i, cc=cc: (i, cc)) for cc in range(N_SLABS)] + [
          pl.BlockSpec((1, d), const),
          pl.BlockSpec((d, c), const, pipeline_mode=pl.Buffered(1)),
          pl.BlockSpec((1, c), const),
      ],
      out_specs=pl.BlockSpec((None, dilation, seg, c), lambda i: (i // tiles, 0, i % tiles, 0)),
      out_shape=jax.ShapeDtypeStruct((batch, dilation, s // dilation, c), BF16),
      scratch_shapes=[pltpu.VMEM((tm, d), BF16)],
      compiler_params=pltpu.CompilerParams(
          dimension_semantics=("parallel",),
          vmem_limit_bytes=_vmem_limit(block_bytes, d * c * 2 + tm * d * 2 + 3 * tm * BRANCH * 4)),
      name=f"in_proj_d{dilation}",
  )(*([x2d] * N_SLABS), norm_gain.reshape(1, d), w_bf16, col_gain.reshape(1, c))


def _pair_attention(q, k, v, bias, want_lse):
  tq = q.shape[0]
  tk = k.shape[0]
  lo = lax.broadcasted_iota(jnp.int32, (tq, LANES), 1) < HEAD_DIM
  zero = jnp.zeros_like(q)
  q2 = jnp.concatenate([jnp.where(lo, q, zero), jnp.where(lo, zero, q)], axis=0)
  s = lax.dot_general(q2, k, (((1,), (1,)), ((), ())), preferred_element_type=F32)
  s = s + bias.reshape(2 * tq, tk)
  m = jnp.max(s, axis=-1, keepdims=True)
  p = jnp.exp2(s - m)
  l = jnp.sum(p, axis=-1, keepdims=True)
  pv = jnp.dot(p.astype(BF16), v, preferred_element_type=F32)
  inv = 1.0 / l
  out = jnp.where(lo, pv[:tq] * inv[:tq], pv[tq:] * inv[tq:])
  if not want_lse:
    return out, None
  lse = (m + jnp.log2(l)) * LN2
  return out, jnp.where(lo, lse[:tq], lse[tq:])


A_TQ = 128
A_TK = 256


def _dilated_attn_kernel(q_ref, k_ref, v_ref, b_ref, o_ref, l_ref, *scratch,
                         n_pairs, seq, spread, tiles_per_store, stores_per_step):
  n_tiles = seq // A_TQ
  rows = tiles_per_store * A_TQ
  seg = rows // spread

  def tile(t, lanes, p):
    q0 = pl.multiple_of(t * A_TQ, A_TQ)
    ws = pl.multiple_of(jnp.clip(q0 - A_TQ // 2, 0, seq - A_TK), A_TQ // 2)
    var = jnp.where(t == 0, 0, jnp.where(t == n_tiles - 1, 2, 1))
    q = q_ref[pl.ds(q0, A_TQ), lanes]
    k = k_ref[pl.ds(ws, A_TK), lanes]
    v = v_ref[pl.ds(ws, A_TK), lanes]
    return _pair_attention(q, k, v, b_ref[var, PAIR * p:PAIR * (p + 1)], True)

  for p in range(n_pairs):
    lanes = _slab(p)

    def step(i, carry, lanes=lanes, p=p):
      for j in range(stores_per_step):
        st = i * stores_per_step + j
        for u in range(tiles_per_store):
          out, lse = tile(st * tiles_per_store + u, lanes, p)
          if spread == 1:
            q0 = pl.multiple_of(st * rows + u * A_TQ, A_TQ)
            o_ref[0, pl.ds(q0, A_TQ), lanes] = out.astype(BF16)
            l_ref[0, pl.ds(q0, A_TQ), lanes] = lse
          else:
            so_ref, sl_ref = scratch
            so_ref[j, u * A_TQ:(u + 1) * A_TQ, :] = out
            sl_ref[j, u * A_TQ:(u + 1) * A_TQ, :] = lse
        if spread > 1:
          base = pl.multiple_of(st * seg, seg)
          for kk in range(spread):
            o_ref[kk, pl.ds(base, seg), lanes] = so_ref[j, pl.ds(kk, seg, stride=spread), :].astype(BF16)
            l_ref[kk, pl.ds(base, seg), lanes] = sl_ref[j, pl.ds(kk, seg, stride=spread), :]
      return carry

    lax.fori_loop(0, n_tiles // (tiles_per_store * stores_per_step), step, 0)


def _t5_bucket(rel):
  half = T5_BUCKETS // 2
  max_exact = half // 2
  ret = jnp.where(rel > 0, half, 0)
  n = jnp.abs(rel)
  nf = jnp.maximum(n, 1).astype(jnp.float32)
  large = max_exact + (jnp.log(nf / max_exact) / math.log(T5_MAX_DISTANCE / max_exact)
                       * (half - max_exact)).astype(jnp.int32)
  large = jnp.minimum(large, half - 1)
  return ret + jnp.where(n < max_exact, n, large)


def _lookup(table, idx, n):
  onehot = (idx[..., None] == jnp.arange(n, dtype=idx.dtype)).astype(F32)
  out = jnp.einsum("...n,pn->p...", onehot, table.reshape(-1, n), precision=lax.Precision.HIGHEST)
  return out.reshape(table.shape[:-1] + idx.shape)


def _dilated_bias(table, dilation, reach):
  a = np.arange(A_TQ)[:, None]
  jj = np.arange(A_TK)[None, :]
  tiles = []
  for delta in (0, -reach, -2 * reach):
    rel = delta + jj - a
    bucket = _t5_bucket(jnp.asarray(rel * dilation, dtype=jnp.int32))
    bias = _lookup(table.astype(F32), bucket, T5_BUCKETS)
    tiles.append(jnp.where(jnp.asarray(np.abs(rel) <= reach), bias * LOG2E, MASK_VALUE))
  return jnp.stack(tiles)


def _dilated_attention(qkv, bias, n_pairs):
  b, d, seq, _ = qkv.shape
  spread = MAX_DILATION // d
  planes = seq // spread
  w = LANES * n_pairs
  n_hb = BRANCH // w
  tiles_per_store = max(1, BF16_ROWS * spread // A_TQ)
  n_stores = seq // (tiles_per_store * A_TQ)
  stores_per_step = min(n_stores, max(1, 4 // tiles_per_store))
  assert (tiles_per_store * A_TQ // spread) % BF16_ROWS == 0 and n_stores % stores_per_step == 0

  def in_map(t):
    return lambda hb, bi, r: (bi, r, 0, t * n_hb + hb)

  out_map = lambda hb, bi, r: (bi, 0, r, 0, hb)
  in_blk = (None, None, seq, w)
  out_blk = (None, spread, None, planes, w)
  block_bytes = seq * w * (3 * 2 + 2 + 4) + 3 * PAIR * n_pairs * A_TQ * A_TK * 4
  scratch = []
  if spread > 1:
    scratch = [pltpu.VMEM((stores_per_step, tiles_per_store * A_TQ, LANES), F32)] * 2
  kern = functools.partial(_dilated_attn_kernel, n_pairs=n_pairs, seq=seq, spread=spread,
                           tiles_per_store=tiles_per_store, stores_per_step=stores_per_step)
  o, lse = pl.pallas_call(
      kern,
      grid=(n_hb, b, d),
      in_specs=[
          pl.BlockSpec(in_blk, in_map(0)),
          pl.BlockSpec(in_blk, in_map(1)),
          pl.BlockSpec(in_blk, in_map(2)),
          pl.BlockSpec((3, PAIR * n_pairs, A_TQ, A_TK), lambda hb, bi, r: (0, hb, 0, 0)),
      ],
      out_specs=[pl.BlockSpec(out_blk, out_map), pl.BlockSpec(out_blk, out_map)],
      out_shape=[jax.ShapeDtypeStruct((b, spread, d, planes, BRANCH), BF16),
                 jax.ShapeDtypeStruct((b, spread, d, planes, BRANCH), F32)],
      scratch_shapes=scratch,
      compiler_params=pltpu.CompilerParams(
          dimension_semantics=("parallel", "parallel", "parallel"),
          vmem_limit_bytes=_vmem_limit(block_bytes, 8 * 2 ** 20)),
      name=f"dilated_attn_d{d}",
  )(qkv, qkv, qkv, bias)
  shape = (b, MAX_DILATION, planes, BRANCH)
  return o.reshape(shape), lse.reshape(shape)


def _na_attn_kernel(q_ref, k_ref, v_ref, b_ref, o_ref, *, rows):
  tk = NA_ROWS * GRID_W

  def row(r, carry):
    rs = jnp.clip(r - NA_ROWS // 2, 0, rows - NA_ROWS)
    q0 = pl.multiple_of(r * GRID_W, GRID_W)
    ws = pl.multiple_of(rs * GRID_W, GRID_W)
    q = q_ref[pl.ds(q0, GRID_W), :]
    k = k_ref[pl.ds(ws, tk), :]
    v = v_ref[pl.ds(ws, tk), :]
    bias = b_ref[rs - r + NA_ROWS - 1]
    out, _ = _pair_attention(q, k, v, bias, False)
    o_ref[pl.ds(q0, GRID_W), :] = out.astype(BF16)
    return carry

  lax.fori_loop(0, rows, row, 0, unroll=4)


def _na_bias(rpb):
  qc = np.arange(GRID_W)[:, None]
  kc = np.arange(GRID_W)[None, :]
  cstart = np.clip(qc - NA_COLS // 2, 0, GRID_W - NA_COLS)
  valid = jnp.asarray((kc >= cstart) & (kc < cstart + NA_COLS))
  col_idx = np.clip(kc - qc, -(NA_COLS - 1), NA_COLS - 1) + NA_COLS - 1
  col_bias = _lookup(rpb.astype(F32), jnp.asarray(col_idx, dtype=jnp.int32), 2 * NA_COLS - 1)
  col_bias = jnp.where(valid[None, None], col_bias * LOG2E, MASK_VALUE)
  variants = []
  for u in range(NA_ROWS):
    rows_u = col_bias[:, u:u + NA_ROWS]
    variants.append(rows_u.transpose(0, 2, 1, 3).reshape(N_HEADS, GRID_W, NA_ROWS * GRID_W))
  return jnp.stack(variants)


def _na_attention(proj, bias):
  b, s, _ = proj.shape
  rows = s // GRID_W
  n_hb = BRANCH // LANES
  blk = (None, s, LANES)
  tk = NA_ROWS * GRID_W
  block_bytes = s * LANES * 2 * 4 + NA_ROWS * PAIR * GRID_W * tk * 4
  return pl.pallas_call(
      functools.partial(_na_attn_kernel, rows=rows),
      grid=(n_hb, b),
      in_specs=[
          pl.BlockSpec(blk, lambda hb, bi: (bi, 0, hb)),
          pl.BlockSpec(blk, lambda hb, bi: (bi, 0, n_hb + hb)),
          pl.BlockSpec(blk, lambda hb, bi: (bi, 0, 2 * n_hb + hb)),
          pl.BlockSpec((NA_ROWS, PAIR, GRID_W, tk), lambda hb, bi: (0, hb, 0, 0)),
      ],
      out_specs=pl.BlockSpec(blk, lambda hb, bi: (bi, 0, hb)),
      out_shape=jax.ShapeDtypeStruct((b, s, BRANCH), BF16),
      compiler_params=pltpu.CompilerParams(
          dimension_semantics=("parallel", "parallel"),
          vmem_limit_bytes=_vmem_limit(block_bytes, 8 * 2 ** 20)),
      name="na_attn",
  )(proj, proj, proj, bias)


def _merge_out_proj_kernel(*refs, n_groups):
  o_refs = refs[:n_groups]
  l_refs = refs[n_groups:2 * n_groups]
  g_ref, x_ref, w_ref, out_ref, y_slab, y_bf = refs[2 * n_groups:]
  seg = g_ref.shape[1]
  for r in range(MAX_DILATION):
    lses = [ref[r] for ref in l_refs]
    mx = functools.reduce(jnp.maximum, lses)
    es = [jnp.exp(l - mx) for l in lses]
    den = functools.reduce(lambda a, c: a + c, es)
    num = functools.reduce(lambda a, c: a + c,
                           [e * ref[r].astype(F32) for e, ref in zip(es, o_refs)])
    y = (num / den) * jax.nn.silu(g_ref[r].astype(F32))
    for c in range(N_SLABS):
      y_slab[c, pl.ds(r, seg, stride=MAX_DILATION), :] = y[:, _slab(c)]
  for c in range(N_SLABS):
    y_bf[:, _slab(c)] = y_slab[c].astype(BF16)
  out_ref[...] = x_ref[...] + jnp.dot(y_bf[...], w_ref[...], preferred_element_type=F32)


def _merge_out_proj(outs, lses, gated, gate_block, x2d, w_bf16, tm=512):
  n, d = x2d.shape
  b, _, planes, _ = outs[0].shape
  seg = tm // MAX_DILATION
  tiles = planes // seg
  n_groups = len(outs)
  assert seg % BF16_ROWS == 0 and planes % seg == 0
  res_blk = (None, MAX_DILATION, seg, BRANCH)
  res_map = lambda i: (i // tiles, 0, i % tiles, 0)
  block_bytes = tm * BRANCH * (2 * n_groups + 4 * n_groups + 2) + 2 * tm * d * 4
  return pl.pallas_call(
      functools.partial(_merge_out_proj_kernel, n_groups=n_groups),
      grid=(n // tm,),
      in_specs=[pl.BlockSpec(res_blk, res_map)] * (2 * n_groups) + [
          pl.BlockSpec(res_blk, lambda i: (i // tiles, 0, i % tiles, gate_block)),
          pl.BlockSpec((tm, d), lambda i: (i, 0)),
          pl.BlockSpec((BRANCH, d), lambda i: (0, 0), pipeline_mode=pl.Buffered(1)),
      ],
      out_specs=pl.BlockSpec((tm, d), lambda i: (i, 0)),
      out_shape=jax.ShapeDtypeStruct((n, d), F32),
      scratch_shapes=[pltpu.VMEM((N_SLABS, tm, LANES), F32), pltpu.VMEM((tm, BRANCH), BF16)],
      compiler_params=pltpu.CompilerParams(
          dimension_semantics=("parallel",),
          vmem_limit_bytes=_vmem_limit(block_bytes, BRANCH * d * 2 + tm * BRANCH * 6 + 8 * 2 ** 20)),
      name="merge_out_proj",
  )(*outs, *lses, gated, x2d, w_bf16)


def _out_proj_kernel(o_ref, g_ref, x_ref, w_ref, out_ref):
  y = o_ref[...].astype(F32) * jax.nn.silu(g_ref[...].astype(F32))
  out_ref[...] = x_ref[...] + jnp.dot(y.astype(BF16), w_ref[...], preferred_element_type=F32)


def _out_proj(o2d, proj2d, gate_block, x2d, w_bf16, tm=512):
  n, d = x2d.shape
  row = lambda i: (i, 0)
  block_bytes = tm * BRANCH * 4 + 2 * tm * d * 4
  return pl.pallas_call(
      _out_proj_kernel,
      grid=(n // tm,),
      in_specs=[
          pl.BlockSpec((tm, BRANCH), row),
          pl.BlockSpec((tm, BRANCH), lambda i: (i, gate_block)),
          pl.BlockSpec((tm, d), row),
          pl.BlockSpec((BRANCH, d), lambda i: (0, 0), pipeline_mode=pl.Buffered(1)),
      ],
      out_specs=pl.BlockSpec((tm, d), row),
      out_shape=jax.ShapeDtypeStruct((n, d), F32),
      compiler_params=pltpu.CompilerParams(
          dimension_semantics=("parallel",),
          vmem_limit_bytes=_vmem_limit(block_bytes, BRANCH * d * 2 + 6 * tm * BRANCH * 4)),
      name="out_proj",
  )(o2d, proj2d, x2d, w_bf16)


QKV_ROLES = (True, True, False)


def _col_gain(q_gain, k_gain):
  q = jnp.tile(q_gain.astype(F32) * (HEAD_DIM ** -0.5 * LOG2E), N_HEADS)
  k = jnp.tile(k_gain.astype(F32), N_HEADS)
  return jnp.concatenate([q, k, jnp.ones((BRANCH,), F32)])


def _layer_a(x, norm_gain, w_in, w_out, q_gain, k_gain, t5_bias):
  b, s, d = x.shape
  x2d = x.reshape(b * s, d)
  ones = jnp.ones((BRANCH,), F32)
  outs, lses, gated = [], [], None
  for g, (window, dilation) in enumerate(DILATED_PAIRS):
    reach = (window // 2) // dilation
    seq = s // dilation
    assert 2 * reach == A_TQ and seq >= A_TK and MAX_DILATION % dilation == 0
    last = g == N_GROUPS - 1
    cols = slice(g * 3 * BRANCH, (g + 1) * 3 * BRANCH + (BRANCH if last else 0))
    gain = _col_gain(q_gain[g], k_gain[g])
    roles = QKV_ROLES + ((False,) if last else ())
    qkv = _in_proj(x2d, b, norm_gain, w_in[:, cols].astype(BF16),
                   jnp.concatenate([gain, ones]) if last else gain, roles, dilation)
    bias = _dilated_bias(t5_bias[g * N_HEADS:(g + 1) * N_HEADS], dilation, reach)
    n_pairs = max(1, min(N_HEADS // PAIR, 4096 // seq))
    o, lse = _dilated_attention(qkv, bias, n_pairs)
    outs.append(o)
    lses.append(lse)
    if last:
      assert dilation == MAX_DILATION
      gated = qkv
  y = _merge_out_proj(outs, lses, gated, 3, x2d, w_out.astype(BF16))
  return y.reshape(b, s, d)


def _layer_b(x, norm_gain, w_in, w_out, q_gain, k_gain, rpb):
  b, s, d = x.shape
  x2d = x.reshape(b * s, d)
  gains = jnp.concatenate([_col_gain(q_gain, k_gain), jnp.ones((BRANCH,), F32)])
  proj = _in_proj(x2d, b, norm_gain, w_in.astype(BF16), gains, QKV_ROLES + (False,), 1)
  proj = proj.reshape(b, s, 4 * BRANCH)
  o = _na_attention(proj, _na_bias(rpb))
  y = _out_proj(o.reshape(b * s, BRANCH), proj.reshape(b * s, 4 * BRANCH), 3, x2d, w_out.astype(BF16))
  return y.reshape(b, s, d)


def kernel(x, norm_gain, a_w_in, a_w_out, a_q_gain, a_k_gain, t5_bias,
           b_w_in, b_w_out, b_q_gain, b_k_gain, b_rpb):
  depth = norm_gain.shape[0]
  for i in range(depth):
    j = i // 2
    if i % 2 == 0:
      x = _layer_a(x, norm_gain[i], a_w_in[j], a_w_out[j], a_q_gain[j], a_k_gain[j], t5_bias)
    else:
      x = _layer_b(x, norm_gain[i], b_w_in[j], b_w_out[j], b_q_gain[j], b_k_gain[j], b_rpb[j])
  return x
```

```python
import functools
import math

import numpy as np
import jax
import jax.numpy as jnp
from jax import lax
from jax.experimental import pallas as pl
from jax.experimental.pallas import tpu as pltpu

D_MODEL = 1024
HEAD_DIM = 64
N_HEADS = D_MODEL // HEAD_DIM
BRANCH = N_HEADS * HEAD_DIM
DILATED_PAIRS = ((128, 1), (512, 4), (2048, 16))
N_GROUPS = len(DILATED_PAIRS)
MAX_DILATION = max(d for _, d in DILATED_PAIRS)
T5_BUCKETS = 32
T5_MAX_DISTANCE = 1024
GRID_W = 64
NA_ROWS = 8
NA_COLS = 16
RMS_EPS = 1e-6
MASK_VALUE = -1e30
LOG2E = math.log2(math.e)
LN2 = math.log(2.0)

LANES = 128
BF16_ROWS = 16
V7X_VMEM_BYTES = 64 * 2 ** 20
PAIR = LANES // HEAD_DIM
N_SLABS = D_MODEL // LANES
N_PAIRS = BRANCH // LANES

F32 = jnp.float32
BF16 = jnp.bfloat16


def _vmem_limit(block_bytes, extra_bytes):
  need = 2 * block_bytes + extra_bytes
  return int(min(need + need // 4, V7X_VMEM_BYTES - 8 * 2 ** 20))


def _slab(c):
  return slice(c * LANES, (c + 1) * LANES)


def _head_norm(a, lo, gain):
  a2 = a * a
  s_lo = jnp.sum(jnp.where(lo, a2, 0.0), axis=-1, keepdims=True)
  s_hi = jnp.sum(jnp.where(lo, 0.0, a2), axis=-1, keepdims=True)
  r_lo = lax.rsqrt(s_lo * (1.0 / HEAD_DIM) + RMS_EPS)
  r_hi = lax.rsqrt(s_hi * (1.0 / HEAD_DIM) + RMS_EPS)
  return (a * jnp.where(lo, r_lo, r_hi)) * gain


def _in_proj_kernel(*refs, dilation, norm_roles):
  x_refs = refs[:N_SLABS]
  ng_ref, w_ref, cg_ref, o_ref, hn_ref = refs[N_SLABS:]
  tm = x_refs[0].shape[0]
  seg = tm // dilation

  for r in range(dilation):
    if dilation == 1:
      xs = [ref[...] for ref in x_refs]
    else:
      xs = [ref[pl.ds(r, seg, stride=dilation), :] for ref in x_refs]
    sq = functools.reduce(lambda a, b: a + b, [x * x for x in xs])
    inv = lax.rsqrt(jnp.sum(sq, axis=-1, keepdims=True) * (1.0 / D_MODEL) + RMS_EPS)
    for c in range(N_SLABS):
      hn_ref[r * seg:(r + 1) * seg, _slab(c)] = ((xs[c] * inv) * ng_ref[:, _slab(c)]).astype(BF16)

  lo = lax.broadcasted_iota(jnp.int32, (tm, LANES), 1) < HEAD_DIM
  for cb, is_norm in enumerate(norm_roles):
    acc = jnp.dot(hn_ref[...], w_ref[:, cb * BRANCH:(cb + 1) * BRANCH],
                  preferred_element_type=F32)
    for c in range(N_PAIRS):
      blk = cb * N_PAIRS + c
      a = acc[:, _slab(c)]
      y = (_head_norm(a, lo, cg_ref[:, _slab(blk)]) if is_norm else a).astype(BF16)
      for r in range(dilation):
        o_ref[r, blk] = y[r * seg:(r + 1) * seg]


def _in_proj(x2d, batch, norm_gain, w_bf16, col_gain, norm_roles, dilation, tm=512):
  n, d = x2d.shape
  c = w_bf16.shape[1]
  s = n // batch
  tiles = s // tm
  seg = tm // dilation
  assert s % tm == 0 and seg % BF16_ROWS == 0 and c == BRANCH * len(norm_roles)
  const = lambda i: (0, 0)
  block_bytes = tm * d * 4 + tm * c * 2 + (d + c) * 4
  kern = functools.partial(_in_proj_kernel, dilation=dilation, norm_roles=norm_roles)
  return pl.pallas_call(
      kern,
      grid=(n // tm,),
      in_specs=[pl.BlockSpec((tm, LANES), lambda i, cc=cc: (i, cc)) for cc in range(N_SLABS)] + [
          pl.BlockSpec((1, d), const),
          pl.BlockSpec((d, c), const, pipeline_mode=pl.Buffered(1)),
          pl.BlockSpec((1, c), const),
      ],
      out_specs=pl.BlockSpec((None, dilation, c // LANES, seg, LANES),
                             lambda i: (i // tiles, 0, 0, i % tiles, 0)),
      out_shape=jax.ShapeDtypeStruct((batch, dilation, c // LANES, s // dilation, LANES), BF16),
      scratch_shapes=[pltpu.VMEM((tm, d), BF16)],
      compiler_params=pltpu.CompilerParams(
          dimension_semantics=("parallel",),
          vmem_limit_bytes=_vmem_limit(block_bytes, d * c * 2 + tm * d * 2 + 3 * tm * BRANCH * 4)),
      name=f"in_proj_d{dilation}",
  )(*([x2d] * N_SLABS), norm_gain.reshape(1, d), w_bf16, col_gain.reshape(1, c))


GROUP = 4


def _divmod_pow2(x, n):
  assert n & (n - 1) == 0
  return lax.shift_right_logical(x, n.bit_length() - 1), x & (n - 1)


def _pipeline(n_groups, stage1, stage2, stage3):
  assert n_groups % 2 == 0 and n_groups >= 2
  stage1(0, 0)
  stage1(1, 1)
  stage2(0, 0)

  def step(j, carry):
    g = 2 * j + 2
    stage1(g, 0)
    stage2(g - 1, 1)
    stage3(g - 2, 0)
    stage1(g + 1, 1)
    stage2(g, 0)
    stage3(g - 1, 1)
    return carry

  lax.fori_loop(0, (n_groups - 2) // 2, step, 0)
  stage2(n_groups - 1, 1)
  stage3(n_groups - 2, 0)
  stage3(n_groups - 1, 1)


def _attention_stages(q_ref, k_ref, v_ref, b_ref, bufs, *, tq, tk, tiles_per_pair, geometry, emit):
  lo = lax.broadcasted_iota(jnp.int32, (tq, LANES), 1) < HEAD_DIM

  def locate(g, u):
    pair, t = _divmod_pow2(jnp.asarray(g, jnp.int32) * GROUP + u, tiles_per_pair)
    return pair, geometry(t)

  def stage1(g, par):
    s_buf = bufs[par][0]
    for u in range(GROUP):
      pair, (q0, k0, var) = locate(g, u)
      q = q_ref[pair, pl.ds(q0, tq), :]
      zero = jnp.zeros_like(q)
      q2 = jnp.concatenate([jnp.where(lo, q, zero), jnp.where(lo, zero, q)], axis=0)
      k = k_ref[pair, pl.ds(k0, tk), :]
      s = lax.dot_general(q2, k, (((1,), (1,)), ((), ())), preferred_element_type=F32)
      s_buf[u] = s + b_ref[var, pair]

  def stage2(g, par):
    s_buf, p_buf, i_buf, l_buf = bufs[par]
    for u in range(GROUP):
      m = jnp.max(s_buf[u], axis=-1, keepdims=True)
      p = jnp.exp2(s_buf[u] - m)
      l = jnp.sum(p, axis=-1, keepdims=True)
      p_buf[u] = p.astype(BF16)
      l_pair = jnp.where(lo, l[:tq], l[tq:])
      i_buf[u] = 1.0 / l_pair
      if l_buf is not None:
        l_buf[u] = (jnp.where(lo, m[:tq], m[tq:]) + jnp.log2(l_pair)) * LN2

  def stage3(g, par):
    _, p_buf, i_buf, l_buf = bufs[par]
    for u in range(GROUP):
      pair, (_, k0, _) = locate(g, u)
      v = v_ref[pair, pl.ds(k0, tk), :]
      pv = jnp.dot(p_buf[u], v, preferred_element_type=F32)
      out = jnp.where(lo, pv[:tq], pv[tq:]) * i_buf[u]
      emit(g, u, out, None if l_buf is None else l_buf[u])

  return stage1, stage2, stage3


def _handoff_scratch(tq, tk, with_lse):
  one = [pltpu.VMEM((GROUP, 2 * tq, tk), F32), pltpu.VMEM((GROUP, 2 * tq, tk), BF16),
         pltpu.VMEM((GROUP, tq, LANES), F32)]
  if with_lse:
    one.append(pltpu.VMEM((GROUP, tq, LANES), F32))
  return one + one


def _split_handoff(scratch, with_lse):
  n = 4 if with_lse else 3
  bufs = []
  for par in range(2):
    b = list(scratch[par * n:(par + 1) * n])
    bufs.append(tuple(b) if with_lse else tuple(b) + (None,))
  return bufs, scratch[2 * n:]


A_TQ = 128
A_TK = 256


def _dilated_attn_kernel(q_ref, k_ref, v_ref, b_ref, o_ref, l_ref, *scratch,
                         n_pairs, seq, spread, tiles_per_store):
  bufs, rest = _split_handoff(scratch, True)
  n_tiles = seq // A_TQ
  rows = tiles_per_store * A_TQ
  seg = rows // spread

  def geometry(t):
    q0 = pl.multiple_of(t * A_TQ, A_TQ)
    k0 = pl.multiple_of(jnp.clip(q0 - A_TQ // 2, 0, seq - A_TK), A_TQ // 2)
    var = jnp.where(t == 0, 0, jnp.where(t == n_tiles - 1, 2, 1))
    return q0, k0, var

  def emit(g, u, out, lse):
    pair, t = _divmod_pow2(jnp.asarray(g, jnp.int32) * GROUP + u, n_tiles)
    if spread == 1:
      q0 = pl.multiple_of(t * A_TQ, A_TQ)
      o_ref[0, pair, pl.ds(q0, A_TQ), :] = out.astype(BF16)
      l_ref[0, pair, pl.ds(q0, A_TQ), :] = lse
      return
    so_ref, sl_ref = rest
    slot, w = divmod(u, tiles_per_store)
    so_ref[slot, w * A_TQ:(w + 1) * A_TQ, :] = out
    sl_ref[slot, w * A_TQ:(w + 1) * A_TQ, :] = lse
    if w == tiles_per_store - 1:
      base = pl.multiple_of(_divmod_pow2(t, tiles_per_store)[0] * seg, seg)
      for kk in range(spread):
        o_ref[kk, pair, pl.ds(base, seg), :] = so_ref[slot, pl.ds(kk, seg, stride=spread), :].astype(BF16)
        l_ref[kk, pair, pl.ds(base, seg), :] = sl_ref[slot, pl.ds(kk, seg, stride=spread), :]

  stages = _attention_stages(q_ref, k_ref, v_ref, b_ref, bufs, tq=A_TQ, tk=A_TK,
                             tiles_per_pair=n_tiles, geometry=geometry, emit=emit)
  _pipeline(n_pairs * n_tiles // GROUP, *stages)


def _t5_bucket(rel):
  half = T5_BUCKETS // 2
  max_exact = half // 2
  ret = jnp.where(rel > 0, half, 0)
  n = jnp.abs(rel)
  nf = jnp.maximum(n, 1).astype(jnp.float32)
  large = max_exact + (jnp.log(nf / max_exact) / math.log(T5_MAX_DISTANCE / max_exact)
                       * (half - max_exact)).astype(jnp.int32)
  large = jnp.minimum(large, half - 1)
  return ret + jnp.where(n < max_exact, n, large)


def _lookup(table, idx, n):
  onehot = (idx[..., None] == jnp.arange(n, dtype=idx.dtype)).astype(F32)
  out = jnp.einsum("...n,pn->p...", onehot, table.reshape(-1, n), precision=lax.Precision.HIGHEST)
  return out.reshape(table.shape[:-1] + idx.shape)


def _stack_pairs(bias):
  h, tq, tk = bias.shape[-3:]
  return bias.reshape(bias.shape[:-3] + (h // PAIR, PAIR * tq, tk))


def _dilated_bias(table, dilation, reach):
  a = np.arange(A_TQ)[:, None]
  jj = np.arange(A_TK)[None, :]
  tiles = []
  for delta in (0, -reach, -2 * reach):
    rel = delta + jj - a
    bucket = _t5_bucket(jnp.asarray(rel * dilation, dtype=jnp.int32))
    bias = _lookup(table.astype(F32), bucket, T5_BUCKETS)
    tiles.append(jnp.where(jnp.asarray(np.abs(rel) <= reach), bias * LOG2E, MASK_VALUE))
  return _stack_pairs(jnp.stack(tiles))


def _dilated_attention(qkv, bias, n_pairs):
  b, d, _, seq, _ = qkv.shape
  spread = MAX_DILATION // d
  planes = seq // spread
  n_hb = N_PAIRS // n_pairs
  n_tiles = seq // A_TQ
  tiles_per_store = max(1, BF16_ROWS * spread // A_TQ)
  assert (tiles_per_store * A_TQ // spread) % BF16_ROWS == 0 and GROUP % tiles_per_store == 0
  assert n_tiles % tiles_per_store == 0 and (n_pairs * n_tiles) % (2 * GROUP) == 0
  assert n_tiles % GROUP == 0 or GROUP % n_tiles == 0

  def in_map(t):
    return lambda hb, bi, r: (bi, r, t * n_hb + hb, 0, 0)

  out_map = lambda hb, bi, r: (bi, 0, r, hb, 0, 0)
  in_blk = (None, None, n_pairs, seq, LANES)
  out_blk = (None, spread, None, n_pairs, planes, LANES)
  block_bytes = n_pairs * seq * LANES * (3 * 2 + 2 + 4) + 3 * n_pairs * PAIR * A_TQ * A_TK * 4
  scratch = _handoff_scratch(A_TQ, A_TK, True)
  if spread > 1:
    scratch += [pltpu.VMEM((GROUP // tiles_per_store, tiles_per_store * A_TQ, LANES), F32)] * 2
  kern = functools.partial(_dilated_attn_kernel, n_pairs=n_pairs, seq=seq, spread=spread,
                           tiles_per_store=tiles_per_store)
  o, lse = pl.pallas_call(
      kern,
      grid=(n_hb, b, d),
      in_specs=[
          pl.BlockSpec(in_blk, in_map(0)),
          pl.BlockSpec(in_blk, in_map(1)),
          pl.BlockSpec(in_blk, in_map(2)),
          pl.BlockSpec((3, n_pairs, PAIR * A_TQ, A_TK), lambda hb, bi, r: (0, hb, 0, 0)),
      ],
      out_specs=[pl.BlockSpec(out_blk, out_map), pl.BlockSpec(out_blk, out_map)],
      out_shape=[jax.ShapeDtypeStruct((b, spread, d, N_PAIRS, planes, LANES), BF16),
                 jax.ShapeDtypeStruct((b, spread, d, N_PAIRS, planes, LANES), F32)],
      scratch_shapes=scratch,
      compiler_params=pltpu.CompilerParams(
          dimension_semantics=("parallel", "parallel", "parallel"),
          vmem_limit_bytes=_vmem_limit(block_bytes, 12 * 2 ** 20)),
      name=f"dilated_attn_d{d}",
  )(qkv, qkv, qkv, bias)
  shape = (b, MAX_DILATION, N_PAIRS, planes, LANES)
  return o.reshape(shape), lse.reshape(shape)


NA_TK = NA_ROWS * GRID_W


def _na_attn_kernel(q_ref, k_ref, v_ref, b_ref, o_ref, *scratch, rows):
  bufs, _ = _split_handoff(scratch, False)

  def geometry(r):
    rs = jnp.clip(r - NA_ROWS // 2, 0, rows - NA_ROWS)
    return (pl.multiple_of(r * GRID_W, GRID_W), pl.multiple_of(rs * GRID_W, GRID_W),
            rs - r + NA_ROWS - 1)

  def emit(g, u, out, _):
    q0 = pl.multiple_of((jnp.asarray(g, jnp.int32) * GROUP + u) * GRID_W, GRID_W)
    o_ref[0, pl.ds(q0, GRID_W), :] = out.astype(BF16)

  stages = _attention_stages(q_ref, k_ref, v_ref, b_ref, bufs, tq=GRID_W, tk=NA_TK,
                             tiles_per_pair=rows, geometry=geometry, emit=emit)
  _pipeline(rows // GROUP, *stages)


def _na_bias(rpb):
  qc = np.arange(GRID_W)[:, None]
  kc = np.arange(GRID_W)[None, :]
  cstart = np.clip(qc - NA_COLS // 2, 0, GRID_W - NA_COLS)
  valid = jnp.asarray((kc >= cstart) & (kc < cstart + NA_COLS))
  col_idx = np.clip(kc - qc, -(NA_COLS - 1), NA_COLS - 1) + NA_COLS - 1
  col_bias = _lookup(rpb.astype(F32), jnp.asarray(col_idx, dtype=jnp.int32), 2 * NA_COLS - 1)
  col_bias = jnp.where(valid[None, None], col_bias * LOG2E, MASK_VALUE)
  variants = []
  for u in range(NA_ROWS):
    rows_u = col_bias[:, u:u + NA_ROWS]
    variants.append(rows_u.transpose(0, 2, 1, 3).reshape(N_HEADS, GRID_W, NA_TK))
  return _stack_pairs(jnp.stack(variants))


def _na_attention(proj, bias):
  b, _, s, _ = proj.shape
  rows = s // GRID_W
  blk = (None, 1, s, LANES)
  assert rows % (2 * GROUP) == 0
  block_bytes = s * LANES * 2 * 4 + NA_ROWS * PAIR * GRID_W * NA_TK * 4
  return pl.pallas_call(
      functools.partial(_na_attn_kernel, rows=rows),
      grid=(N_PAIRS, b),
      in_specs=[
          pl.BlockSpec(blk, lambda hb, bi: (bi, hb, 0, 0)),
          pl.BlockSpec(blk, lambda hb, bi: (bi, N_PAIRS + hb, 0, 0)),
          pl.BlockSpec(blk, lambda hb, bi: (bi, 2 * N_PAIRS + hb, 0, 0)),
          pl.BlockSpec((NA_ROWS, 1, PAIR * GRID_W, NA_TK), lambda hb, bi: (0, hb, 0, 0)),
      ],
      out_specs=pl.BlockSpec(blk, lambda hb, bi: (bi, hb, 0, 0)),
      out_shape=jax.ShapeDtypeStruct((b, N_PAIRS, s, LANES), BF16),
      scratch_shapes=_handoff_scratch(GRID_W, NA_TK, False),
      compiler_params=pltpu.CompilerParams(
          dimension_semantics=("parallel", "parallel"),
          vmem_limit_bytes=_vmem_limit(block_bytes, 12 * 2 ** 20)),
      name="na_attn",
  )(proj, proj, proj, bias)


def _merge_out_proj_kernel(*refs, n_groups):
  o_refs = refs[:n_groups]
  l_refs = refs[n_groups:2 * n_groups]
  g_ref, x_ref, w_ref, out_ref, y_slab, y_bf = refs[2 * n_groups:]
  seg = g_ref.shape[2]
  for r in range(MAX_DILATION):
    for c in range(N_PAIRS):
      lses = [ref[r, c] for ref in l_refs]
      mx = functools.reduce(jnp.maximum, lses)
      es = [jnp.exp(l - mx) for l in lses]
      den = functools.reduce(lambda a, b: a + b, es)
      num = functools.reduce(lambda a, b: a + b,
                             [e * ref[r, c].astype(F32) for e, ref in zip(es, o_refs)])
      y = (num / den) * jax.nn.silu(g_ref[r, c].astype(F32))
      y_slab[c, pl.ds(r, seg, stride=MAX_DILATION), :] = y
  for c in range(N_PAIRS):
    y_bf[:, _slab(c)] = y_slab[c].astype(BF16)
  out_ref[...] = x_ref[...] + jnp.dot(y_bf[...], w_ref[...], preferred_element_type=F32)


def _merge_out_proj(outs, lses, gated, gate_block, x2d, w_bf16, tm=512):
  n, d = x2d.shape
  b, _, _, planes, _ = outs[0].shape
  seg = tm // MAX_DILATION
  tiles = planes // seg
  n_groups = len(outs)
  assert seg % BF16_ROWS == 0 and planes % seg == 0
  res_blk = (None, MAX_DILATION, N_PAIRS, seg, LANES)
  res_map = lambda i: (i // tiles, 0, 0, i % tiles, 0)
  block_bytes = tm * BRANCH * (2 * n_groups + 4 * n_groups + 2) + 2 * tm * d * 4
  return pl.pallas_call(
      functools.partial(_merge_out_proj_kernel, n_groups=n_groups),
      grid=(n // tm,),
      in_specs=[pl.BlockSpec(res_blk, res_map)] * (2 * n_groups) + [
          pl.BlockSpec(res_blk, lambda i: (i // tiles, 0, gate_block, i % tiles, 0)),
          pl.BlockSpec((tm, d), lambda i: (i, 0)),
          pl.BlockSpec((BRANCH, d), lambda i: (0, 0), pipeline_mode=pl.Buffered(1)),
      ],
      out_specs=pl.BlockSpec((tm, d), lambda i: (i, 0)),
      out_shape=jax.ShapeDtypeStruct((n, d), F32),
      scratch_shapes=[pltpu.VMEM((N_PAIRS, tm, LANES), F32), pltpu.VMEM((tm, BRANCH), BF16)],
      compiler_params=pltpu.CompilerParams(
          dimension_semantics=("parallel",),
          vmem_limit_bytes=_vmem_limit(block_bytes, BRANCH * d * 2 + tm * BRANCH * 6 + 8 * 2 ** 20)),
      name="merge_out_proj",
  )(*outs, *lses, gated, x2d, w_bf16)


def _out_proj_kernel(o_ref, g_ref, x_ref, w_ref, out_ref, y_bf):
  for c in range(N_PAIRS):
    y = o_ref[c].astype(F32) * jax.nn.silu(g_ref[c].astype(F32))
    y_bf[:, _slab(c)] = y.astype(BF16)
  out_ref[...] = x_ref[...] + jnp.dot(y_bf[...], w_ref[...], preferred_element_type=F32)


def _out_proj(o, proj, gate_block, x2d, w_bf16, tm=512):
  n, d = x2d.shape
  b, _, s, _ = o.shape
  tiles = s // tm
  blk = (None, N_PAIRS, tm, LANES)
  block_bytes = tm * BRANCH * 4 + 2 * tm * d * 4
  return pl.pallas_call(
      _out_proj_kernel,
      grid=(n // tm,),
      in_specs=[
          pl.BlockSpec(blk, lambda i: (i // tiles, 0, i % tiles, 0)),
          pl.BlockSpec(blk, lambda i: (i // tiles, gate_block, i % tiles, 0)),
          pl.BlockSpec((tm, d), lambda i: (i, 0)),
          pl.BlockSpec((BRANCH, d), lambda i: (0, 0), pipeline_mode=pl.Buffered(1)),
      ],
      out_specs=pl.BlockSpec((tm, d), lambda i: (i, 0)),
      out_shape=jax.ShapeDtypeStruct((n, d), F32),
      scratch_shapes=[pltpu.VMEM((tm, BRANCH), BF16)],
      compiler_params=pltpu.CompilerParams(
          dimension_semantics=("parallel",),
          vmem_limit_bytes=_vmem_limit(block_bytes, BRANCH * d * 2 + 6 * tm * BRANCH * 4)),
      name="out_proj",
  )(o, proj, x2d, w_bf16)


QKV_ROLES = (True, True, False)


def _col_gain(q_gain, k_gain):
  q = jnp.tile(q_gain.astype(F32) * (HEAD_DIM ** -0.5 * LOG2E), N_HEADS)
  k = jnp.tile(k_gain.astype(F32), N_HEADS)
  return jnp.concatenate([q, k, jnp.ones((BRANCH,), F32)])


def _layer_a(x, norm_gain, w_in, w_out, q_gain, k_gain, t5_bias):
  b, s, d = x.shape
  x2d = x.reshape(b * s, d)
  ones = jnp.ones((BRANCH,), F32)
  outs, lses, gated = [], [], None
  for g, (window, dilation) in enumerate(DILATED_PAIRS):
    reach = (window // 2) // dilation
    seq = s // dilation
    assert 2 * reach == A_TQ and seq >= A_TK and MAX_DILATION % dilation == 0
    last = g == N_GROUPS - 1
    cols = slice(g * 3 * BRANCH, (g + 1) * 3 * BRANCH + (BRANCH if last else 0))
    gain = _col_gain(q_gain[g], k_gain[g])
    roles = QKV_ROLES + ((False,) if last else ())
    qkv = _in_proj(x2d, b, norm_gain, w_in[:, cols].astype(BF16),
                   jnp.concatenate([gain, ones]) if last else gain, roles, dilation)
    bias = _dilated_bias(t5_bias[g * N_HEADS:(g + 1) * N_HEADS], dilation, reach)
    n_pairs = max(1, min(N_PAIRS, 4096 // seq))
    o, lse = _dilated_attention(qkv, bias, n_pairs)
    outs.append(o)
    lses.append(lse)
    if last:
      assert dilation == MAX_DILATION
      gated = qkv
  y = _merge_out_proj(outs, lses, gated, 3, x2d, w_out.astype(BF16))
  return y.reshape(b, s, d)


def _layer_b(x, norm_gain, w_in, w_out, q_gain, k_gain, rpb):
  b, s, d = x.shape
  x2d = x.reshape(b * s, d)
  gains = jnp.concatenate([_col_gain(q_gain, k_gain), jnp.ones((BRANCH,), F32)])
  proj = _in_proj(x2d, b, norm_gain, w_in.astype(BF16), gains, QKV_ROLES + (False,), 1)
  proj = proj.reshape(b, 4 * N_PAIRS, s, LANES)
  o = _na_attention(proj, _na_bias(rpb))
  y = _out_proj(o, proj, 3, x2d, w_out.astype(BF16))
  return y.reshape(b, s, d)


def kernel(x, norm_gain, a_w_in, a_w_out, a_q_gain, a_k_gain, t5_bias,
           b_w_in, b_w_out, b_q_gain, b_k_gain, b_rpb):
  depth = norm_gain.shape[0]
  for i in range(depth):
    j = i // 2
    if i % 2 == 0:
      x = _layer_a(x, norm_gain[i], a_w_in[j], a_w_out[j], a_q_gain[j], a_k_gain[j], t5_bias)
    else:
      x = _layer_b(x, norm_gain[i], b_w_in[j], b_w_out[j], b_q_gain[j], b_k_gain[j], b_rpb[j])
  return x
```

```python
import functools
import math

import numpy as np
import jax
import jax.numpy as jnp
from jax import lax
from jax.experimental import pallas as pl
from jax.experimental.pallas import tpu as pltpu

D_MODEL = 1024
HEAD_DIM = 64
N_HEADS = D_MODEL // HEAD_DIM
BRANCH = N_HEADS * HEAD_DIM
DILATED_PAIRS = ((128, 1), (512, 4), (2048, 16))
N_GROUPS = len(DILATED_PAIRS)
MAX_DILATION = max(d for _, d in DILATED_PAIRS)
T5_BUCKETS = 32
T5_MAX_DISTANCE = 1024
GRID_W = 64
NA_ROWS = 8
NA_COLS = 16
RMS_EPS = 1e-6
MASK_VALUE = -1e30
LOG2E = math.log2(math.e)
LN2 = math.log(2.0)

LANES = 128
BF16_ROWS = 16
V7X_VMEM_BYTES = 64 * 2 ** 20
PAIR = LANES // HEAD_DIM
N_SLABS = D_MODEL // LANES
N_PAIRS = BRANCH // LANES

F32 = jnp.float32
BF16 = jnp.bfloat16


def _vmem_limit(block_bytes, extra_bytes):
  need = 2 * block_bytes + extra_bytes
  return int(min(need + need // 4, V7X_VMEM_BYTES - 8 * 2 ** 20))


def _slab(c):
  return slice(c * LANES, (c + 1) * LANES)


def _head_norm(a, lo, gain):
  a2 = a * a
  s_lo = jnp.sum(jnp.where(lo, a2, 0.0), axis=-1, keepdims=True)
  s_hi = jnp.sum(jnp.where(lo, 0.0, a2), axis=-1, keepdims=True)
  r_lo = lax.rsqrt(s_lo * (1.0 / HEAD_DIM) + RMS_EPS)
  r_hi = lax.rsqrt(s_hi * (1.0 / HEAD_DIM) + RMS_EPS)
  return (a * jnp.where(lo, r_lo, r_hi)) * gain


def _in_proj_kernel(*refs, dilation, norm_roles):
  x_refs = refs[:N_SLABS]
  ng_ref, w_ref, cg_ref, o_ref, hn_ref = refs[N_SLABS:]
  tm = x_refs[0].shape[0]
  seg = tm // dilation

  for r in range(dilation):
    if dilation == 1:
      xs = [ref[...] for ref in x_refs]
    else:
      xs = [ref[pl.ds(r, seg, stride=dilation), :] for ref in x_refs]
    sq = functools.reduce(lambda a, b: a + b, [x * x for x in xs])
    inv = lax.rsqrt(jnp.sum(sq, axis=-1, keepdims=True) * (1.0 / D_MODEL) + RMS_EPS)
    for c in range(N_SLABS):
      hn_ref[r * seg:(r + 1) * seg, _slab(c)] = ((xs[c] * inv) * ng_ref[:, _slab(c)]).astype(BF16)
  _project(hn_ref, w_ref, cg_ref, o_ref, dilation, norm_roles)


def _project(hn_ref, w_ref, cg_ref, o_ref, dilation, norm_roles):
  tm = hn_ref.shape[0]
  seg = tm // dilation
  lo = lax.broadcasted_iota(jnp.int32, (tm, LANES), 1) < HEAD_DIM
  for cb, is_norm in enumerate(norm_roles):
    acc = jnp.dot(hn_ref[...], w_ref[:, cb * BRANCH:(cb + 1) * BRANCH],
                  preferred_element_type=F32)
    for c in range(N_PAIRS):
      blk = cb * N_PAIRS + c
      a = acc[:, _slab(c)]
      y = (_head_norm(a, lo, cg_ref[:, _slab(blk)]) if is_norm else a).astype(BF16)
      for r in range(dilation):
        o_ref[r, blk] = y[r * seg:(r + 1) * seg]


def _in_proj(x2d, batch, norm_gain, w_bf16, col_gain, norm_roles, dilation, tm=512):
  n, d = x2d.shape
  c = w_bf16.shape[1]
  s = n // batch
  tiles = s // tm
  seg = tm // dilation
  assert s % tm == 0 and seg % BF16_ROWS == 0 and c == BRANCH * len(norm_roles)
  const = lambda i: (0, 0)
  block_bytes = tm * d * 4 + tm * c * 2 + (d + c) * 4
  kern = functools.partial(_in_proj_kernel, dilation=dilation, norm_roles=norm_roles)
  return pl.pallas_call(
      kern,
      grid=(n // tm,),
      in_specs=[pl.BlockSpec((tm, LANES), lambda i, cc=cc: (i, cc)) for cc in range(N_SLABS)] + [
          pl.BlockSpec((1, d), const),
          pl.BlockSpec((d, c), const, pipeline_mode=pl.Buffered(1)),
          pl.BlockSpec((1, c), const),
      ],
      out_specs=pl.BlockSpec((None, dilation, c // LANES, seg, LANES),
                             lambda i: (i // tiles, 0, 0, i % tiles, 0)),
      out_shape=jax.ShapeDtypeStruct((batch, dilation, c // LANES, s // dilation, LANES), BF16),
      scratch_shapes=[pltpu.VMEM((tm, d), BF16)],
      compiler_params=pltpu.CompilerParams(
          dimension_semantics=("parallel",),
          vmem_limit_bytes=_vmem_limit(block_bytes, d * c * 2 + tm * d * 2 + 3 * tm * BRANCH * 4)),
      name=f"in_proj_d{dilation}",
  )(*([x2d] * N_SLABS), norm_gain.reshape(1, d), w_bf16, col_gain.reshape(1, c))


GROUP = 4


def _divmod_pow2(x, n):
  assert n & (n - 1) == 0
  return lax.shift_right_logical(x, n.bit_length() - 1), x & (n - 1)


def _pipeline(n_groups, stage1, stage2, stage3):
  assert n_groups % 2 == 0 and n_groups >= 2
  stage1(0, 0)
  stage1(1, 1)
  stage2(0, 0)

  def step(j, carry):
    g = 2 * j + 2
    stage1(g, 0)
    stage2(g - 1, 1)
    stage3(g - 2, 0)
    stage1(g + 1, 1)
    stage2(g, 0)
    stage3(g - 1, 1)
    return carry

  lax.fori_loop(0, (n_groups - 2) // 2, step, 0)
  stage2(n_groups - 1, 1)
  stage3(n_groups - 2, 0)
  stage3(n_groups - 1, 1)


def _attention_stages(q_ref, k_ref, v_ref, b_ref, bufs, *, tq, tk, tiles_per_pair, geometry, emit):
  lo = lax.broadcasted_iota(jnp.int32, (tq, LANES), 1) < HEAD_DIM

  def locate(g, u):
    pair, t = _divmod_pow2(jnp.asarray(g, jnp.int32) * GROUP + u, tiles_per_pair)
    return pair, geometry(t)

  def stage1(g, par):
    s_buf = bufs[par][0]
    for u in range(GROUP):
      pair, (q0, k0, var) = locate(g, u)
      q = q_ref[pair, pl.ds(q0, tq), :]
      zero = jnp.zeros_like(q)
      q2 = jnp.concatenate([jnp.where(lo, q, zero), jnp.where(lo, zero, q)], axis=0)
      k = k_ref[pair, pl.ds(k0, tk), :]
      s = lax.dot_general(q2, k, (((1,), (1,)), ((), ())), preferred_element_type=F32)
      s_buf[u] = s + b_ref[var, pair]

  def stage2(g, par):
    s_buf, p_buf, m_buf = bufs[par]
    for u in range(GROUP):
      m = jnp.max(s_buf[u], axis=-1, keepdims=True)
      p_buf[u] = jnp.exp2(s_buf[u] - m).astype(BF16)
      if m_buf is not None:
        m_buf[u] = jnp.where(lo, m[:tq], m[tq:])

  ones = jnp.ones((tk, LANES), BF16)

  def stage3(g, par):
    _, p_buf, m_buf = bufs[par]
    for u in range(GROUP):
      pair, (_, k0, _) = locate(g, u)
      v1 = jnp.concatenate([v_ref[pair, pl.ds(k0, tk), :], ones], axis=1)
      pv = jnp.dot(p_buf[u], v1, preferred_element_type=F32)
      acc = jnp.where(lo, pv[:tq, :LANES], pv[tq:, :LANES])
      l = jnp.where(lo, pv[:tq, LANES:], pv[tq:, LANES:])
      lse = None if m_buf is None else (m_buf[u] + jnp.log2(l)) * LN2
      emit(g, u, acc * (1.0 / l), lse)

  return stage1, stage2, stage3


def _handoff_scratch(tq, tk, with_lse):
  one = [pltpu.VMEM((GROUP, 2 * tq, tk), F32), pltpu.VMEM((GROUP, 2 * tq, tk), BF16)]
  if with_lse:
    one.append(pltpu.VMEM((GROUP, tq, LANES), F32))
  return one + one


def _split_handoff(scratch, with_lse):
  n = 3 if with_lse else 2
  bufs = []
  for par in range(2):
    b = tuple(scratch[par * n:(par + 1) * n])
    bufs.append(b if with_lse else b + (None,))
  return bufs, scratch[2 * n:]


A_TQ = 128
A_TK = 256


def _dilated_attn_kernel(q_ref, k_ref, v_ref, b_ref, o_ref, l_ref, *scratch,
                         n_pairs, seq, spread, tiles_per_store):
  bufs, rest = _split_handoff(scratch, True)
  n_tiles = seq // A_TQ
  rows = tiles_per_store * A_TQ
  seg = rows // spread

  def geometry(t):
    q0 = pl.multiple_of(t * A_TQ, A_TQ)
    k0 = pl.multiple_of(jnp.clip(q0 - A_TQ // 2, 0, seq - A_TK), A_TQ // 2)
    var = jnp.where(t == 0, 0, jnp.where(t == n_tiles - 1, 2, 1))
    return q0, k0, var

  def emit(g, u, out, lse):
    pair, t = _divmod_pow2(jnp.asarray(g, jnp.int32) * GROUP + u, n_tiles)
    if spread == 1:
      q0 = pl.multiple_of(t * A_TQ, A_TQ)
      o_ref[0, pair, pl.ds(q0, A_TQ), :] = out.astype(BF16)
      l_ref[0, pair, pl.ds(q0, A_TQ), :] = lse
      return
    so_ref, sl_ref = rest
    slot, w = divmod(u, tiles_per_store)
    so_ref[slot, w * A_TQ:(w + 1) * A_TQ, :] = out
    sl_ref[slot, w * A_TQ:(w + 1) * A_TQ, :] = lse
    if w == tiles_per_store - 1:
      base = pl.multiple_of(_divmod_pow2(t, tiles_per_store)[0] * seg, seg)
      for kk in range(spread):
        o_ref[kk, pair, pl.ds(base, seg), :] = so_ref[slot, pl.ds(kk, seg, stride=spread), :].astype(BF16)
        l_ref[kk, pair, pl.ds(base, seg), :] = sl_ref[slot, pl.ds(kk, seg, stride=spread), :]

  stages = _attention_stages(q_ref, k_ref, v_ref, b_ref, bufs, tq=A_TQ, tk=A_TK,
                             tiles_per_pair=n_tiles, geometry=geometry, emit=emit)
  _pipeline(n_pairs * n_tiles // GROUP, *stages)


def _t5_bucket(rel):
  half = T5_BUCKETS // 2
  max_exact = half // 2
  ret = jnp.where(rel > 0, half, 0)
  n = jnp.abs(rel)
  nf = jnp.maximum(n, 1).astype(jnp.float32)
  large = max_exact + (jnp.log(nf / max_exact) / math.log(T5_MAX_DISTANCE / max_exact)
                       * (half - max_exact)).astype(jnp.int32)
  large = jnp.minimum(large, half - 1)
  return ret + jnp.where(n < max_exact, n, large)


def _lookup(table, idx, n):
  onehot = (idx[..., None] == jnp.arange(n, dtype=idx.dtype)).astype(F32)
  out = jnp.einsum("...n,pn->p...", onehot, table.reshape(-1, n), precision=lax.Precision.HIGHEST)
  return out.reshape(table.shape[:-1] + idx.shape)


def _stack_pairs(bias):
  h, tq, tk = bias.shape[-3:]
  return bias.reshape(bias.shape[:-3] + (h // PAIR, PAIR * tq, tk))


def _dilated_bias(table, dilation, reach):
  a = np.arange(A_TQ)[:, None]
  jj = np.arange(A_TK)[None, :]
  tiles = []
  for delta in (0, -reach, -2 * reach):
    rel = delta + jj - a
    bucket = _t5_bucket(jnp.asarray(rel * dilation, dtype=jnp.int32))
    bias = _lookup(table.astype(F32), bucket, T5_BUCKETS)
    tiles.append(jnp.where(jnp.asarray(np.abs(rel) <= reach), bias * LOG2E, MASK_VALUE))
  return _stack_pairs(jnp.stack(tiles))


def _dilated_attention(qkv, bias, n_pairs):
  b, d, _, seq, _ = qkv.shape
  spread = MAX_DILATION // d
  planes = seq // spread
  n_hb = N_PAIRS // n_pairs
  n_tiles = seq // A_TQ
  tiles_per_store = max(1, BF16_ROWS * spread // A_TQ)
  assert (tiles_per_store * A_TQ // spread) % BF16_ROWS == 0 and GROUP % tiles_per_store == 0
  assert n_tiles % tiles_per_store == 0 and (n_pairs * n_tiles) % (2 * GROUP) == 0
  assert n_tiles % GROUP == 0 or GROUP % n_tiles == 0

  def in_map(t):
    return lambda hb, bi, r: (bi, r, t * n_hb + hb, 0, 0)

  out_map = lambda hb, bi, r: (bi, 0, r, hb, 0, 0)
  in_blk = (None, None, n_pairs, seq, LANES)
  out_blk = (None, spread, None, n_pairs, planes, LANES)
  block_bytes = n_pairs * seq * LANES * (3 * 2 + 2 + 4) + 3 * n_pairs * PAIR * A_TQ * A_TK * 4
  scratch = _handoff_scratch(A_TQ, A_TK, True)
  if spread > 1:
    scratch += [pltpu.VMEM((GROUP // tiles_per_store, tiles_per_store * A_TQ, LANES), F32)] * 2
  kern = functools.partial(_dilated_attn_kernel, n_pairs=n_pairs, seq=seq, spread=spread,
                           tiles_per_store=tiles_per_store)
  o, lse = pl.pallas_call(
      kern,
      grid=(n_hb, b, d),
      in_specs=[
          pl.BlockSpec(in_blk, in_map(0)),
          pl.BlockSpec(in_blk, in_map(1)),
          pl.BlockSpec(in_blk, in_map(2)),
          pl.BlockSpec((3, n_pairs, PAIR * A_TQ, A_TK), lambda hb, bi, r: (0, hb, 0, 0)),
      ],
      out_specs=[pl.BlockSpec(out_blk, out_map), pl.BlockSpec(out_blk, out_map)],
      out_shape=[jax.ShapeDtypeStruct((b, spread, d, N_PAIRS, planes, LANES), BF16),
                 jax.ShapeDtypeStruct((b, spread, d, N_PAIRS, planes, LANES), F32)],
      scratch_shapes=scratch,
      compiler_params=pltpu.CompilerParams(
          dimension_semantics=("parallel", "parallel", "parallel"),
          vmem_limit_bytes=_vmem_limit(block_bytes, 12 * 2 ** 20)),
      name=f"dilated_attn_d{d}",
  )(qkv, qkv, qkv, bias)
  shape = (b, MAX_DILATION, N_PAIRS, planes, LANES)
  return o.reshape(shape), lse.reshape(shape)


NA_TK = NA_ROWS * GRID_W


def _na_attn_kernel(q_ref, k_ref, v_ref, b_ref, o_ref, *scratch, rows):
  bufs, _ = _split_handoff(scratch, False)

  def geometry(r):
    rs = jnp.clip(r - NA_ROWS // 2, 0, rows - NA_ROWS)
    return (pl.multiple_of(r * GRID_W, GRID_W), pl.multiple_of(rs * GRID_W, GRID_W),
            rs - r + NA_ROWS - 1)

  def emit(g, u, out, _):
    q0 = pl.multiple_of((jnp.asarray(g, jnp.int32) * GROUP + u) * GRID_W, GRID_W)
    o_ref[0, pl.ds(q0, GRID_W), :] = out.astype(BF16)

  stages = _attention_stages(q_ref, k_ref, v_ref, b_ref, bufs, tq=GRID_W, tk=NA_TK,
                             tiles_per_pair=rows, geometry=geometry, emit=emit)
  _pipeline(rows // GROUP, *stages)


def _na_bias(rpb):
  qc = np.arange(GRID_W)[:, None]
  kc = np.arange(GRID_W)[None, :]
  cstart = np.clip(qc - NA_COLS // 2, 0, GRID_W - NA_COLS)
  valid = jnp.asarray((kc >= cstart) & (kc < cstart + NA_COLS))
  col_idx = np.clip(kc - qc, -(NA_COLS - 1), NA_COLS - 1) + NA_COLS - 1
  col_bias = _lookup(rpb.astype(F32), jnp.asarray(col_idx, dtype=jnp.int32), 2 * NA_COLS - 1)
  col_bias = jnp.where(valid[None, None], col_bias * LOG2E, MASK_VALUE)
  variants = []
  for u in range(NA_ROWS):
    rows_u = col_bias[:, u:u + NA_ROWS]
    variants.append(rows_u.transpose(0, 2, 1, 3).reshape(N_HEADS, GRID_W, NA_TK))
  return _stack_pairs(jnp.stack(variants))


def _na_attention(proj, bias):
  b, _, s, _ = proj.shape
  rows = s // GRID_W
  blk = (None, 1, s, LANES)
  assert rows % (2 * GROUP) == 0
  block_bytes = s * LANES * 2 * 4 + NA_ROWS * PAIR * GRID_W * NA_TK * 4
  return pl.pallas_call(
      functools.partial(_na_attn_kernel, rows=rows),
      grid=(N_PAIRS, b),
      in_specs=[
          pl.BlockSpec(blk, lambda hb, bi: (bi, hb, 0, 0)),
          pl.BlockSpec(blk, lambda hb, bi: (bi, N_PAIRS + hb, 0, 0)),
          pl.BlockSpec(blk, lambda hb, bi: (bi, 2 * N_PAIRS + hb, 0, 0)),
          pl.BlockSpec((NA_ROWS, 1, PAIR * GRID_W, NA_TK), lambda hb, bi: (0, hb, 0, 0)),
      ],
      out_specs=pl.BlockSpec(blk, lambda hb, bi: (bi, hb, 0, 0)),
      out_shape=jax.ShapeDtypeStruct((b, N_PAIRS, s, LANES), BF16),
      scratch_shapes=_handoff_scratch(GRID_W, NA_TK, False),
      compiler_params=pltpu.CompilerParams(
          dimension_semantics=("parallel", "parallel"),
          vmem_limit_bytes=_vmem_limit(block_bytes, 12 * 2 ** 20)),
      name="na_attn",
  )(proj, proj, proj, bias)


def _merge_out_proj_kernel(*refs, n_groups, next_roles):
  o_refs = refs[:n_groups]
  l_refs = refs[n_groups:2 * n_groups]
  (g_ref, x_ref, w_ref, ng_ref, w2_ref, cg2_ref, out_ref, proj_ref,
   y_slab, y_bf, hn_ref) = refs[2 * n_groups:]
  seg = g_ref.shape[2]
  for r in range(MAX_DILATION):
    for c in range(N_PAIRS):
      lses = [ref[r, c] for ref in l_refs]
      mx = functools.reduce(jnp.maximum, lses)
      es = [jnp.exp(l - mx) for l in lses]
      den = functools.reduce(lambda a, b: a + b, es)
      num = functools.reduce(lambda a, b: a + b,
                             [e * ref[r, c].astype(F32) for e, ref in zip(es, o_refs)])
      y = (num / den) * jax.nn.silu(g_ref[r, c].astype(F32))
      y_slab[c, pl.ds(r, seg, stride=MAX_DILATION), :] = y
  for c in range(N_PAIRS):
    y_bf[:, _slab(c)] = y_slab[c].astype(BF16)
  xn = x_ref[...] + jnp.dot(y_bf[...], w_ref[...], preferred_element_type=F32)
  out_ref[...] = xn
  inv = lax.rsqrt(jnp.mean(xn * xn, axis=-1, keepdims=True) + RMS_EPS)
  hn_ref[...] = ((xn * inv) * ng_ref[...]).astype(BF16)
  _project(hn_ref, w2_ref, cg2_ref, proj_ref, 1, next_roles)


def _merge_out_proj(outs, lses, gated, gate_block, x2d, w_bf16,
                    next_norm_gain, next_w_bf16, next_col_gain, next_roles, tm=256):
  n, d = x2d.shape
  b, _, _, planes, _ = outs[0].shape
  s = planes * MAX_DILATION
  c2 = next_w_bf16.shape[1]
  seg = tm // MAX_DILATION
  tiles = planes // seg
  n_groups = len(outs)
  assert seg % BF16_ROWS == 0 and planes % seg == 0 and c2 == BRANCH * len(next_roles)
  res_blk = (None, MAX_DILATION, N_PAIRS, seg, LANES)
  res_map = lambda i: (i // tiles, 0, 0, i % tiles, 0)
  const = lambda i: (0, 0)
  block_bytes = tm * BRANCH * (2 * n_groups + 4 * n_groups + 2) + 2 * tm * d * 4 + tm * c2 * 2
  return pl.pallas_call(
      functools.partial(_merge_out_proj_kernel, n_groups=n_groups, next_roles=next_roles),
      grid=(n // tm,),
      in_specs=[pl.BlockSpec(res_blk, res_map)] * (2 * n_groups) + [
          pl.BlockSpec(res_blk, lambda i: (i // tiles, 0, gate_block, i % tiles, 0)),
          pl.BlockSpec((tm, d), lambda i: (i, 0)),
          pl.BlockSpec((BRANCH, d), const, pipeline_mode=pl.Buffered(1)),
          pl.BlockSpec((1, d), const),
          pl.BlockSpec((d, c2), const, pipeline_mode=pl.Buffered(1)),
          pl.BlockSpec((1, c2), const),
      ],
      out_specs=[pl.BlockSpec((tm, d), lambda i: (i, 0)),
                 pl.BlockSpec((None, 1, c2 // LANES, tm, LANES), lambda i: (i // tiles, 0, 0, i % tiles, 0))],
      out_shape=[jax.ShapeDtypeStruct((n, d), F32),
                 jax.ShapeDtypeStruct((b, 1, c2 // LANES, s, LANES), BF16)],
      scratch_shapes=[pltpu.VMEM((N_PAIRS, tm, LANES), F32), pltpu.VMEM((tm, BRANCH), BF16),
                      pltpu.VMEM((tm, d), BF16)],
      compiler_params=pltpu.CompilerParams(
          dimension_semantics=("parallel",),
          vmem_limit_bytes=_vmem_limit(block_bytes, (BRANCH * d + d * c2) * 2 + tm * BRANCH * 8
                                       + 3 * tm * BRANCH * 4 + 4 * 2 ** 20)),
      name="merge_out_in_proj",
  )(*outs, *lses, gated, x2d, w_bf16, next_norm_gain.reshape(1, d), next_w_bf16,
    next_col_gain.reshape(1, c2))


def _out_proj_kernel(o_ref, g_ref, x_ref, w_ref, out_ref, y_bf):
  for c in range(N_PAIRS):
    y = o_ref[c].astype(F32) * jax.nn.silu(g_ref[c].astype(F32))
    y_bf[:, _slab(c)] = y.astype(BF16)
  out_ref[...] = x_ref[...] + jnp.dot(y_bf[...], w_ref[...], preferred_element_type=F32)


def _out_proj(o, proj, gate_block, x2d, w_bf16, tm=512):
  n, d = x2d.shape
  b, _, s, _ = o.shape
  tiles = s // tm
  blk = (None, N_PAIRS, tm, LANES)
  block_bytes = tm * BRANCH * 4 + 2 * tm * d * 4
  return pl.pallas_call(
      _out_proj_kernel,
      grid=(n // tm,),
      in_specs=[
          pl.BlockSpec(blk, lambda i: (i // tiles, 0, i % tiles, 0)),
          pl.BlockSpec(blk, lambda i: (i // tiles, gate_block, i % tiles, 0)),
          pl.BlockSpec((tm, d), lambda i: (i, 0)),
          pl.BlockSpec((BRANCH, d), lambda i: (0, 0), pipeline_mode=pl.Buffered(1)),
      ],
      out_specs=pl.BlockSpec((tm, d), lambda i: (i, 0)),
      out_shape=jax.ShapeDtypeStruct((n, d), F32),
      scratch_shapes=[pltpu.VMEM((tm, BRANCH), BF16)],
      compiler_params=pltpu.CompilerParams(
          dimension_semantics=("parallel",),
          vmem_limit_bytes=_vmem_limit(block_bytes, BRANCH * d * 2 + 6 * tm * BRANCH * 4)),
      name="out_proj",
  )(o, proj, x2d, w_bf16)


QKV_ROLES = (True, True, False)


def _col_gain(q_gain, k_gain):
  q = jnp.tile(q_gain.astype(F32) * (HEAD_DIM ** -0.5 * LOG2E), N_HEADS)
  k = jnp.tile(k_gain.astype(F32), N_HEADS)
  return jnp.concatenate([q, k, jnp.ones((BRANCH,), F32)])


def _layer_a(x, norm_gain, w_in, w_out, q_gain, k_gain, t5_bias, next_in_proj):
  b, s, d = x.shape
  x2d = x.reshape(b * s, d)
  ones = jnp.ones((BRANCH,), F32)
  outs, lses, gated = [], [], None
  for g, (window, dilation) in enumerate(DILATED_PAIRS):
    reach = (window // 2) // dilation
    seq = s // dilation
    assert 2 * reach == A_TQ and seq >= A_TK and MAX_DILATION % dilation == 0
    last = g == N_GROUPS - 1
    cols = slice(g * 3 * BRANCH, (g + 1) * 3 * BRANCH + (BRANCH if last else 0))
    gain = _col_gain(q_gain[g], k_gain[g])
    roles = QKV_ROLES + ((False,) if last else ())
    qkv = _in_proj(x2d, b, norm_gain, w_in[:, cols].astype(BF16),
                   jnp.concatenate([gain, ones]) if last else gain, roles, dilation)
    bias = _dilated_bias(t5_bias[g * N_HEADS:(g + 1) * N_HEADS], dilation, reach)
    n_pairs = max(1, min(N_PAIRS, 4096 // seq))
    o, lse = _dilated_attention(qkv, bias, n_pairs)
    outs.append(o)
    lses.append(lse)
    if last:
      assert dilation == MAX_DILATION
      gated = qkv
  y, proj = _merge_out_proj(outs, lses, gated, 3, x2d, w_out.astype(BF16), *next_in_proj)
  return y.reshape(b, s, d), proj


B_ROLES = QKV_ROLES + (False,)


def _b_in_proj_args(norm_gain, w_in, q_gain, k_gain):
  gains = jnp.concatenate([_col_gain(q_gain, k_gain), jnp.ones((BRANCH,), F32)])
  return norm_gain, w_in.astype(BF16), gains, B_ROLES


def _layer_b(x, proj, w_out, rpb):
  b, s, d = x.shape
  proj = proj.reshape(b, 4 * N_PAIRS, s, LANES)
  o = _na_attention(proj, _na_bias(rpb))
  y = _out_proj(o, proj, 3, x.reshape(b * s, d), w_out.astype(BF16))
  return y.reshape(b, s, d)


def kernel(x, norm_gain, a_w_in, a_w_out, a_q_gain, a_k_gain, t5_bias,
           b_w_in, b_w_out, b_q_gain, b_k_gain, b_rpb):
  assert norm_gain.shape[0] == 2 and a_w_in.shape[0] == 1 and b_w_in.shape[0] == 1
  x, proj = _layer_a(x, norm_gain[0], a_w_in[0], a_w_out[0], a_q_gain[0], a_k_gain[0], t5_bias,
                     _b_in_proj_args(norm_gain[1], b_w_in[0], b_q_gain[0], b_k_gain[0]))
  return _layer_b(x, proj, b_w_out[0], b_rpb[0])
```

```python
import functools
import math

import numpy as np
import jax
import jax.numpy as jnp
from jax import lax
from jax.experimental import pallas as pl
from jax.experimental.pallas import tpu as pltpu

D_MODEL = 1024
HEAD_DIM = 64
N_HEADS = D_MODEL // HEAD_DIM
BRANCH = N_HEADS * HEAD_DIM
DILATED_PAIRS = ((128, 1), (512, 4), (2048, 16))
N_GROUPS = len(DILATED_PAIRS)
MAX_DILATION = max(d for _, d in DILATED_PAIRS)
T5_BUCKETS = 32
T5_MAX_DISTANCE = 1024
GRID_W = 64
NA_ROWS = 8
NA_COLS = 16
RMS_EPS = 1e-6
MASK_VALUE = -1e30
LOG2E = math.log2(math.e)

LANES = 128
BF16_ROWS = 16
V7X_VMEM_BYTES = 64 * 2 ** 20
PAIR = LANES // HEAD_DIM
N_SLABS = D_MODEL // LANES
N_PAIRS = BRANCH // LANES

F32 = jnp.float32
BF16 = jnp.bfloat16


def _vmem_limit(block_bytes, extra_bytes):
  need = 2 * block_bytes + extra_bytes
  return int(min(need + need // 4, V7X_VMEM_BYTES - 8 * 2 ** 20))


def _slab(c):
  return slice(c * LANES, (c + 1) * LANES)


def _head_norm(a, lo, gain):
  a2 = a * a
  s_lo = jnp.sum(jnp.where(lo, a2, 0.0), axis=-1, keepdims=True)
  s_hi = jnp.sum(jnp.where(lo, 0.0, a2), axis=-1, keepdims=True)
  inv = lax.rsqrt(jnp.where(lo, s_lo, s_hi) + HEAD_DIM * RMS_EPS)
  return (a * inv) * gain


def _in_proj_kernel(*refs, dilation, roles):
  x_refs = refs[:N_SLABS]
  ng_ref, w_ref, cg_ref, o_ref, hn_ref = refs[N_SLABS:]
  tm = x_refs[0].shape[0]
  seg = tm // dilation

  for r in range(dilation):
    if dilation == 1:
      xs = [ref[...] for ref in x_refs]
    else:
      xs = [ref[pl.ds(r, seg, stride=dilation), :] for ref in x_refs]
    sq = functools.reduce(lambda a, b: a + b, [x * x for x in xs])
    inv = lax.rsqrt(jnp.sum(sq, axis=-1, keepdims=True) * (1.0 / D_MODEL) + RMS_EPS)
    for c in range(N_SLABS):
      hn_ref[r * seg:(r + 1) * seg, _slab(c)] = ((xs[c] * inv) * ng_ref[:, _slab(c)]).astype(BF16)
  _project(hn_ref, w_ref, cg_ref, o_ref, dilation, roles)


NORM, PLAIN, SILU = "norm", "plain", "silu"


def _project(hn_ref, w_ref, cg_ref, o_ref, dilation, roles):
  tm = hn_ref.shape[0]
  seg = tm // dilation
  lo = lax.broadcasted_iota(jnp.int32, (tm, LANES), 1) < HEAD_DIM
  for cb, role in enumerate(roles):
    acc = jnp.dot(hn_ref[...], w_ref[:, cb * BRANCH:(cb + 1) * BRANCH],
                  preferred_element_type=F32)
    for c in range(N_PAIRS):
      blk = cb * N_PAIRS + c
      a = acc[:, _slab(c)]
      if role == NORM:
        a = _head_norm(a, lo, cg_ref[:, _slab(blk)])
      elif role == SILU:
        a = jax.nn.silu(a)
      y = a.astype(BF16)
      for r in range(dilation):
        o_ref[r, blk] = y[r * seg:(r + 1) * seg]


def _in_proj(x2d, batch, norm_gain, w_bf16, col_gain, roles, dilation, tm=512):
  n, d = x2d.shape
  c = w_bf16.shape[1]
  s = n // batch
  tiles = s // tm
  seg = tm // dilation
  assert s % tm == 0 and seg % BF16_ROWS == 0 and c == BRANCH * len(roles)
  const = lambda i: (0, 0)
  block_bytes = tm * d * 4 + tm * c * 2 + (d + c) * 4
  kern = functools.partial(_in_proj_kernel, dilation=dilation, roles=roles)
  return pl.pallas_call(
      kern,
      grid=(n // tm,),
      in_specs=[pl.BlockSpec((tm, LANES), lambda i, cc=cc: (i, cc)) for cc in range(N_SLABS)] + [
          pl.BlockSpec((1, d), const),
          pl.BlockSpec((d, c), const, pipeline_mode=pl.Buffered(1)),
          pl.BlockSpec((1, c), const),
      ],
      out_specs=pl.BlockSpec((None, dilation, c // LANES, seg, LANES),
                             lambda i: (i // tiles, 0, 0, i % tiles, 0)),
      out_shape=jax.ShapeDtypeStruct((batch, dilation, c // LANES, s // dilation, LANES), BF16),
      scratch_shapes=[pltpu.VMEM((tm, d), BF16)],
      compiler_params=pltpu.CompilerParams(
          dimension_semantics=("parallel",),
          vmem_limit_bytes=_vmem_limit(block_bytes, d * c * 2 + tm * d * 2 + 3 * tm * BRANCH * 4)),
      name=f"in_proj_d{dilation}",
  )(*([x2d] * N_SLABS), norm_gain.reshape(1, d), w_bf16, col_gain.reshape(1, c))


GROUP = 4


def _divmod_pow2(x, n):
  assert n & (n - 1) == 0
  return lax.shift_right_logical(x, n.bit_length() - 1), x & (n - 1)


def _pipeline(n_groups, stage1, stage2, stage3):
  assert n_groups % 2 == 0 and n_groups >= 2
  stage1(0, 0)
  stage1(1, 1)
  stage2(0, 0)

  def step(j, carry):
    g = 2 * j + 2
    stage1(g, 0)
    stage2(g - 1, 1)
    stage3(g - 2, 0)
    stage1(g + 1, 1)
    stage2(g, 0)
    stage3(g - 1, 1)
    return carry

  lax.fori_loop(0, (n_groups - 2) // 2, step, 0)
  stage2(n_groups - 1, 1)
  stage3(n_groups - 2, 0)
  stage3(n_groups - 1, 1)


def _attention_stages(q_ref, k_ref, v_ref, b_ref, bufs, *, tq, tk, tiles_per_pair, geometry, emit):
  lo = lax.broadcasted_iota(jnp.int32, (tq, LANES), 1) < HEAD_DIM

  def locate(g, u):
    pair, t = _divmod_pow2(jnp.asarray(g, jnp.int32) * GROUP + u, tiles_per_pair)
    return pair, geometry(t)

  def stage1(g, par):
    s_buf = bufs[par][0]
    for u in range(GROUP):
      pair, (q0, k0, var) = locate(g, u)
      q = q_ref[pair, pl.ds(q0, tq), :]
      zero = jnp.zeros_like(q)
      q2 = jnp.concatenate([jnp.where(lo, q, zero), jnp.where(lo, zero, q)], axis=0)
      k = k_ref[pair, pl.ds(k0, tk), :]
      s = lax.dot_general(q2, k, (((1,), (1,)), ((), ())), preferred_element_type=F32)
      s_buf[u] = s + b_ref[var, pair]

  def stage2(g, par):
    s_buf, p_buf, m_buf = bufs[par]
    for u in range(GROUP):
      m = jnp.max(s_buf[u], axis=-1, keepdims=True)
      p_buf[u] = jnp.exp2(s_buf[u] - m).astype(BF16)
      if m_buf is not None:
        m_buf[u] = jnp.where(lo, m[:tq], m[tq:])

  ones = jnp.ones((tk, LANES), BF16)

  def stage3(g, par):
    _, p_buf, m_buf = bufs[par]
    for u in range(GROUP):
      pair, (_, k0, _) = locate(g, u)
      v1 = jnp.concatenate([v_ref[pair, pl.ds(k0, tk), :], ones], axis=1)
      pv = jnp.dot(p_buf[u], v1, preferred_element_type=F32)
      acc = jnp.where(lo, pv[:tq, :LANES], pv[tq:, :LANES])
      l = jnp.where(lo, pv[:tq, LANES:], pv[tq:, LANES:])
      lse = None if m_buf is None else m_buf[u] + jnp.log2(l)
      emit(g, u, acc * (1.0 / l), lse)

  return stage1, stage2, stage3


def _handoff_scratch(tq, tk, with_lse):
  one = [pltpu.VMEM((GROUP, 2 * tq, tk), F32), pltpu.VMEM((GROUP, 2 * tq, tk), BF16)]
  if with_lse:
    one.append(pltpu.VMEM((GROUP, tq, LANES), F32))
  return one + one


def _split_handoff(scratch, with_lse):
  n = 3 if with_lse else 2
  bufs = []
  for par in range(2):
    b = tuple(scratch[par * n:(par + 1) * n])
    bufs.append(b if with_lse else b + (None,))
  return bufs, scratch[2 * n:]


A_TQ = 128
A_TK = 256


def _dilated_attn_kernel(q_ref, k_ref, v_ref, b_ref, o_ref, l_ref, *scratch,
                         n_pairs, seq, spread, tiles_per_store):
  bufs, rest = _split_handoff(scratch, True)
  n_tiles = seq // A_TQ
  rows = tiles_per_store * A_TQ
  seg = rows // spread

  def geometry(t):
    q0 = pl.multiple_of(t * A_TQ, A_TQ)
    k0 = pl.multiple_of(jnp.clip(q0 - A_TQ // 2, 0, seq - A_TK), A_TQ // 2)
    var = jnp.where(t == 0, 0, jnp.where(t == n_tiles - 1, 2, 1))
    return q0, k0, var

  def emit(g, u, out, lse):
    pair, t = _divmod_pow2(jnp.asarray(g, jnp.int32) * GROUP + u, n_tiles)
    if spread == 1:
      q0 = pl.multiple_of(t * A_TQ, A_TQ)
      o_ref[0, pair, pl.ds(q0, A_TQ), :] = out.astype(BF16)
      l_ref[0, pair, pl.ds(q0, A_TQ), :] = lse
      return
    so_ref, sl_ref = rest
    slot, w = divmod(u, tiles_per_store)
    so_ref[slot, w * A_TQ:(w + 1) * A_TQ, :] = out
    sl_ref[slot, w * A_TQ:(w + 1) * A_TQ, :] = lse
    if w == tiles_per_store - 1:
      base = pl.multiple_of(_divmod_pow2(t, tiles_per_store)[0] * seg, seg)
      for kk in range(spread):
        o_ref[kk, pair, pl.ds(base, seg), :] = so_ref[slot, pl.ds(kk, seg, stride=spread), :].astype(BF16)
        l_ref[kk, pair, pl.ds(base, seg), :] = sl_ref[slot, pl.ds(kk, seg, stride=spread), :]

  stages = _attention_stages(q_ref, k_ref, v_ref, b_ref, bufs, tq=A_TQ, tk=A_TK,
                             tiles_per_pair=n_tiles, geometry=geometry, emit=emit)
  _pipeline(n_pairs * n_tiles // GROUP, *stages)


def _t5_bucket(rel):
  half = T5_BUCKETS // 2
  max_exact = half // 2
  ret = jnp.where(rel > 0, half, 0)
  n = jnp.abs(rel)
  nf = jnp.maximum(n, 1).astype(jnp.float32)
  large = max_exact + (jnp.log(nf / max_exact) / math.log(T5_MAX_DISTANCE / max_exact)
                       * (half - max_exact)).astype(jnp.int32)
  large = jnp.minimum(large, half - 1)
  return ret + jnp.where(n < max_exact, n, large)


def _stack_pairs(bias):
  h, tq, tk = bias.shape[-3:]
  return bias.reshape(bias.shape[:-3] + (h // PAIR, PAIR * tq, tk))


def _dilated_bias(table, dilation, reach):
  rel = np.arange(-reach, reach + 1)
  band = table[:, _t5_bucket(jnp.asarray(rel * dilation, dtype=jnp.int32))].astype(F32) * LOG2E
  n_diag = A_TQ + A_TK - 1
  padded = jnp.pad(band, ((0, 0), (n_diag, n_diag)), constant_values=MASK_VALUE)
  tiles = []
  for delta in (0, -reach, -2 * reach):
    start = n_diag + reach + delta - (A_TQ - 1)
    tiles.append(_toeplitz(padded[:, start:start + n_diag]))
  return _stack_pairs(jnp.stack(tiles))


def _toeplitz(diag):
  lead = diag.shape[:-1]
  n_diag = diag.shape[-1]
  ext = jnp.concatenate([diag, jnp.full(lead + (1,), MASK_VALUE, diag.dtype)], axis=-1)
  rows = jnp.tile(ext, A_TQ)[..., :A_TQ * n_diag].reshape(lead + (A_TQ, n_diag))
  return rows[..., A_TQ - 1:]


def _dilated_attention(qkv, bias, n_pairs):
  b, d, _, seq, _ = qkv.shape
  spread = MAX_DILATION // d
  planes = seq // spread
  n_hb = N_PAIRS // n_pairs
  n_tiles = seq // A_TQ
  tiles_per_store = max(1, BF16_ROWS * spread // A_TQ)
  assert (tiles_per_store * A_TQ // spread) % BF16_ROWS == 0 and GROUP % tiles_per_store == 0
  assert n_tiles % tiles_per_store == 0 and (n_pairs * n_tiles) % (2 * GROUP) == 0
  assert n_tiles % GROUP == 0 or GROUP % n_tiles == 0

  def in_map(t):
    return lambda hb, bi, r: (bi, r, t * n_hb + hb, 0, 0)

  out_map = lambda hb, bi, r: (bi, 0, r, hb, 0, 0)
  in_blk = (None, None, n_pairs, seq, LANES)
  out_blk = (None, spread, None, n_pairs, planes, LANES)
  block_bytes = n_pairs * seq * LANES * (3 * 2 + 2 + 4) + 3 * n_pairs * PAIR * A_TQ * A_TK * 4
  scratch = _handoff_scratch(A_TQ, A_TK, True)
  if spread > 1:
    scratch += [pltpu.VMEM((GROUP // tiles_per_store, tiles_per_store * A_TQ, LANES), F32)] * 2
  kern = functools.partial(_dilated_attn_kernel, n_pairs=n_pairs, seq=seq, spread=spread,
                           tiles_per_store=tiles_per_store)
  o, lse = pl.pallas_call(
      kern,
      grid=(n_hb, b, d),
      in_specs=[
          pl.BlockSpec(in_blk, in_map(0)),
          pl.BlockSpec(in_blk, in_map(1)),
          pl.BlockSpec(in_blk, in_map(2)),
          pl.BlockSpec((3, n_pairs, PAIR * A_TQ, A_TK), lambda hb, bi, r: (0, hb, 0, 0)),
      ],
      out_specs=[pl.BlockSpec(out_blk, out_map), pl.BlockSpec(out_blk, out_map)],
      out_shape=[jax.ShapeDtypeStruct((b, spread, d, N_PAIRS, planes, LANES), BF16),
                 jax.ShapeDtypeStruct((b, spread, d, N_PAIRS, planes, LANES), F32)],
      scratch_shapes=scratch,
      compiler_params=pltpu.CompilerParams(
          dimension_semantics=("parallel", "parallel", "parallel"),
          vmem_limit_bytes=_vmem_limit(block_bytes, 12 * 2 ** 20)),
      name=f"dilated_attn_d{d}",
  )(qkv, qkv, qkv, bias)
  shape = (b, MAX_DILATION, N_PAIRS, planes, LANES)
  return o.reshape(shape), lse.reshape(shape)


NA_TK = NA_ROWS * GRID_W


def _na_attn_kernel(q_ref, k_ref, v_ref, b_ref, o_ref, *scratch, rows):
  bufs, _ = _split_handoff(scratch, False)

  def geometry(r):
    rs = jnp.clip(r - NA_ROWS // 2, 0, rows - NA_ROWS)
    return (pl.multiple_of(r * GRID_W, GRID_W), pl.multiple_of(rs * GRID_W, GRID_W),
            rs - r + NA_ROWS - 1)

  def emit(g, u, out, _):
    q0 = pl.multiple_of((jnp.asarray(g, jnp.int32) * GROUP + u) * GRID_W, GRID_W)
    o_ref[0, pl.ds(q0, GRID_W), :] = out.astype(BF16)

  stages = _attention_stages(q_ref, k_ref, v_ref, b_ref, bufs, tq=GRID_W, tk=NA_TK,
                             tiles_per_pair=rows, geometry=geometry, emit=emit)
  _pipeline(rows // GROUP, *stages)


def _na_bias(rpb):
  qc = np.arange(GRID_W)[:, None]
  kc = np.arange(GRID_W)[None, :]
  cstart = np.clip(qc - NA_COLS // 2, 0, GRID_W - NA_COLS)
  valid = (kc >= cstart) & (kc < cstart + NA_COLS)
  col_idx = np.clip(kc - qc, -(NA_COLS - 1), NA_COLS - 1) + NA_COLS - 1
  col_onehot = (col_idx[..., None] == np.arange(2 * NA_COLS - 1)).astype(np.float32)
  cols = jnp.einsum("qkc,hjc->hqjk", col_onehot, rpb.astype(F32) * LOG2E,
                    precision=lax.Precision.HIGHEST)
  cols = jnp.where(jnp.asarray(valid)[None, :, None, :], cols, MASK_VALUE)
  cols = cols.reshape(N_HEADS, GRID_W, (2 * NA_ROWS - 1) * GRID_W)
  return _stack_pairs(jnp.stack([cols[:, :, u * GRID_W:u * GRID_W + NA_TK] for u in range(NA_ROWS)]))


def _na_attention(proj, bias):
  b, _, s, _ = proj.shape
  rows = s // GRID_W
  blk = (None, 1, s, LANES)
  assert rows % (2 * GROUP) == 0
  block_bytes = s * LANES * 2 * 4 + NA_ROWS * PAIR * GRID_W * NA_TK * 4
  return pl.pallas_call(
      functools.partial(_na_attn_kernel, rows=rows),
      grid=(N_PAIRS, b),
      in_specs=[
          pl.BlockSpec(blk, lambda hb, bi: (bi, hb, 0, 0)),
          pl.BlockSpec(blk, lambda hb, bi: (bi, N_PAIRS + hb, 0, 0)),
          pl.BlockSpec(blk, lambda hb, bi: (bi, 2 * N_PAIRS + hb, 0, 0)),
          pl.BlockSpec((NA_ROWS, 1, PAIR * GRID_W, NA_TK), lambda hb, bi: (0, hb, 0, 0)),
      ],
      out_specs=pl.BlockSpec(blk, lambda hb, bi: (bi, hb, 0, 0)),
      out_shape=jax.ShapeDtypeStruct((b, N_PAIRS, s, LANES), BF16),
      scratch_shapes=_handoff_scratch(GRID_W, NA_TK, False),
      compiler_params=pltpu.CompilerParams(
          dimension_semantics=("parallel", "parallel"),
          vmem_limit_bytes=_vmem_limit(block_bytes, 12 * 2 ** 20)),
      name="na_attn",
  )(proj, proj, proj, bias)


def _merge_out_proj_kernel(*refs, n_groups, next_roles):
  o_refs = refs[:n_groups]
  l_refs = refs[n_groups:2 * n_groups]
  (g_ref, x_ref, w_ref, ng_ref, w2_ref, cg2_ref, out_ref, proj_ref,
   y_slab, y_bf, hn_ref) = refs[2 * n_groups:]
  seg = g_ref.shape[2]

  @pl.when(pl.program_id(0) == 0)
  def _():
    hn_ref[...] = jnp.zeros_like(hn_ref)

  def merge_plane(r):
    for c in range(N_PAIRS):
      lses = [ref[r, c] for ref in l_refs]
      mx = functools.reduce(jnp.maximum, lses)
      es = [jnp.exp2(l - mx) for l in lses]
      den = functools.reduce(lambda a, b: a + b, es)
      num = functools.reduce(lambda a, b: a + b,
                             [e * ref[r, c].astype(F32) for e, ref in zip(es, o_refs)])
      y = (num / den) * g_ref[r, c].astype(F32)
      y_slab[c, pl.ds(r, seg, stride=MAX_DILATION), :] = y

  _project(hn_ref, w2_ref, cg2_ref, proj_ref, 1, next_roles)
  for r in range(MAX_DILATION):
    merge_plane(r)
  for c in range(N_PAIRS):
    y_bf[:, _slab(c)] = y_slab[c].astype(BF16)
  xn = x_ref[...] + jnp.dot(y_bf[...], w_ref[...], preferred_element_type=F32)
  out_ref[...] = xn
  inv = lax.rsqrt(jnp.mean(xn * xn, axis=-1, keepdims=True) + RMS_EPS)
  hn_ref[...] = ((xn * inv) * ng_ref[...]).astype(BF16)


def _merge_out_proj(outs, lses, gated, gate_block, x2d, w_bf16,
                    next_norm_gain, next_w_bf16, next_col_gain, next_roles, tm=256):
  n, d = x2d.shape
  b, _, _, planes, _ = outs[0].shape
  s = planes * MAX_DILATION
  c2 = next_w_bf16.shape[1]
  seg = tm // MAX_DILATION
  tiles = planes // seg
  n_groups = len(outs)
  assert seg % BF16_ROWS == 0 and planes % seg == 0 and c2 == BRANCH * len(next_roles)
  res_blk = (None, MAX_DILATION, N_PAIRS, seg, LANES)
  n_steps = n // tm
  merged = lambda i: jnp.minimum(i, n_steps - 1)
  projected = lambda i: jnp.maximum(i - 1, 0)
  res_map = lambda i: (merged(i) // tiles, 0, 0, merged(i) % tiles, 0)
  const = lambda i: (0, 0)
  block_bytes = tm * BRANCH * (2 * n_groups + 4 * n_groups + 2) + 2 * tm * d * 4 + tm * c2 * 2
  return pl.pallas_call(
      functools.partial(_merge_out_proj_kernel, n_groups=n_groups, next_roles=next_roles),
      grid=(n_steps + 1,),
      in_specs=[pl.BlockSpec(res_blk, res_map)] * (2 * n_groups) + [
          pl.BlockSpec(res_blk, lambda i: (merged(i) // tiles, 0, gate_block, merged(i) % tiles, 0)),
          pl.BlockSpec((tm, d), lambda i: (merged(i), 0)),
          pl.BlockSpec((BRANCH, d), const, pipeline_mode=pl.Buffered(1)),
          pl.BlockSpec((1, d), const),
          pl.BlockSpec((d, c2), const, pipeline_mode=pl.Buffered(1)),
          pl.BlockSpec((1, c2), const),
      ],
      out_specs=[pl.BlockSpec((tm, d), lambda i: (merged(i), 0)),
                 pl.BlockSpec((None, 1, c2 // LANES, tm, LANES),
                              lambda i: (projected(i) // tiles, 0, 0, projected(i) % tiles, 0))],
      out_shape=[jax.ShapeDtypeStruct((n, d), F32),
                 jax.ShapeDtypeStruct((b, 1, c2 // LANES, s, LANES), BF16)],
      scratch_shapes=[pltpu.VMEM((N_PAIRS, tm, LANES), F32), pltpu.VMEM((tm, BRANCH), BF16),
                      pltpu.VMEM((tm, d), BF16)],
      compiler_params=pltpu.CompilerParams(
          dimension_semantics=("arbitrary",),
          vmem_limit_bytes=_vmem_limit(block_bytes, (BRANCH * d + d * c2) * 2 + tm * BRANCH * 8
                                       + 3 * tm * BRANCH * 4 + 4 * 2 ** 20)),
      name="merge_out_in_proj",
  )(*outs, *lses, gated, x2d, w_bf16, next_norm_gain.reshape(1, d), next_w_bf16,
    next_col_gain.reshape(1, c2))


def _out_proj_kernel(o_ref, g_ref, x_ref, w_ref, out_ref, y_bf):
  for c in range(N_PAIRS):
    y = o_ref[c].astype(F32) * g_ref[c].astype(F32)
    y_bf[:, _slab(c)] = y.astype(BF16)
  out_ref[...] = x_ref[...] + jnp.dot(y_bf[...], w_ref[...], preferred_element_type=F32)


def _out_proj(o, proj, gate_block, x2d, w_bf16, tm=512):
  n, d = x2d.shape
  b, _, s, _ = o.shape
  tiles = s // tm
  blk = (None, N_PAIRS, tm, LANES)
  block_bytes = tm * BRANCH * 4 + 2 * tm * d * 4
  return pl.pallas_call(
      _out_proj_kernel,
      grid=(n // tm,),
      in_specs=[
          pl.BlockSpec(blk, lambda i: (i // tiles, 0, i % tiles, 0)),
          pl.BlockSpec(blk, lambda i: (i // tiles, gate_block, i % tiles, 0)),
          pl.BlockSpec((tm, d), lambda i: (i, 0)),
          pl.BlockSpec((BRANCH, d), lambda i: (0, 0), pipeline_mode=pl.Buffered(1)),
      ],
      out_specs=pl.BlockSpec((tm, d), lambda i: (i, 0)),
      out_shape=jax.ShapeDtypeStruct((n, d), F32),
      scratch_shapes=[pltpu.VMEM((tm, BRANCH), BF16)],
      compiler_params=pltpu.CompilerParams(
          dimension_semantics=("parallel",),
          vmem_limit_bytes=_vmem_limit(block_bytes, BRANCH * d * 2 + 6 * tm * BRANCH * 4)),
      name="out_proj",
  )(o, proj, x2d, w_bf16)


QKV_ROLES = (NORM, NORM, PLAIN)


def _col_gain(q_gain, k_gain):
  q = jnp.tile(q_gain.astype(F32) * LOG2E, N_HEADS)
  k = jnp.tile(k_gain.astype(F32) * HEAD_DIM ** 0.5, N_HEADS)
  return jnp.concatenate([q, k, jnp.ones((BRANCH,), F32)])


def _layer_a(x, norm_gain, w_in, w_out, q_gain, k_gain, t5_bias, next_in_proj):
  b, s, d = x.shape
  x2d = x.reshape(b * s, d)
  ones = jnp.ones((BRANCH,), F32)
  outs, lses, gated = [], [], None
  for g, (window, dilation) in enumerate(DILATED_PAIRS):
    reach = (window // 2) // dilation
    seq = s // dilation
    assert 2 * reach == A_TQ and seq >= A_TK and MAX_DILATION % dilation == 0
    last = g == N_GROUPS - 1
    cols = slice(g * 3 * BRANCH, (g + 1) * 3 * BRANCH + (BRANCH if last else 0))
    gain = _col_gain(q_gain[g], k_gain[g])
    roles = QKV_ROLES + ((SILU,) if last else ())
    qkv = _in_proj(x2d, b, norm_gain, w_in[:, cols].astype(BF16),
                   jnp.concatenate([gain, ones]) if last else gain, roles, dilation)
    bias = _dilated_bias(t5_bias[g * N_HEADS:(g + 1) * N_HEADS], dilation, reach)
    n_pairs = max(1, min(N_PAIRS, 4096 // seq))
    o, lse = _dilated_attention(qkv, bias, n_pairs)
    outs.append(o)
    lses.append(lse)
    if last:
      assert dilation == MAX_DILATION
      gated = qkv
  y, proj = _merge_out_proj(outs, lses, gated, 3, x2d, w_out.astype(BF16), *next_in_proj)
  return y.reshape(b, s, d), proj


B_ROLES = QKV_ROLES + (SILU,)


def _b_in_proj_args(norm_gain, w_in, q_gain, k_gain):
  gains = jnp.concatenate([_col_gain(q_gain, k_gain), jnp.ones((BRANCH,), F32)])
  return norm_gain, w_in.astype(BF16), gains, B_ROLES


def _layer_b(x, proj, w_out, rpb):
  b, s, d = x.shape
  proj = proj.reshape(b, 4 * N_PAIRS, s, LANES)
  o = _na_attention(proj, _na_bias(rpb))
  y = _out_proj(o, proj, 3, x.reshape(b * s, d), w_out.astype(BF16))
  return y.reshape(b, s, d)


def kernel(x, norm_gain, a_w_in, a_w_out, a_q_gain, a_k_gain, t5_bias,
           b_w_in, b_w_out, b_q_gain, b_k_gain, b_rpb):
  assert norm_gain.shape[0] == 2 and a_w_in.shape[0] == 1 and b_w_in.shape[0] == 1
  x, proj = _layer_a(x, norm_gain[0], a_w_in[0], a_w_out[0], a_q_gain[0], a_k_gain[0], t5_bias,
                     _b_in_proj_args(norm_gain[1], b_w_in[0], b_q_gain[0], b_k_gain[0]))
  return _layer_b(x, proj, b_w_out[0], b_rpb[0])
```

```python
import functools
import math

import numpy as np
import jax
import jax.numpy as jnp
from jax import lax
from jax.experimental import pallas as pl
from jax.experimental.pallas import tpu as pltpu

D_MODEL = 1024
HEAD_DIM = 64
N_HEADS = D_MODEL // HEAD_DIM
BRANCH = N_HEADS * HEAD_DIM
DILATED_PAIRS = ((128, 1), (512, 4), (2048, 16))
N_GROUPS = len(DILATED_PAIRS)
MAX_DILATION = max(d for _, d in DILATED_PAIRS)
T5_BUCKETS = 32
T5_MAX_DISTANCE = 1024
GRID_W = 64
NA_ROWS = 8
NA_COLS = 16
RMS_EPS = 1e-6
MASK_VALUE = -1e30
LOG2E = math.log2(math.e)

LANES = 128
BF16_ROWS = 16
V7X_VMEM_BYTES = 64 * 2 ** 20
PAIR = LANES // HEAD_DIM
N_SLABS = D_MODEL // LANES
N_PAIRS = BRANCH // LANES
DEINTERLEAVE_STRIDE = 4

F32 = jnp.float32
BF16 = jnp.bfloat16


def _vmem_limit(block_bytes, extra_bytes):
  need = 2 * block_bytes + extra_bytes
  return int(min(need + need // 4, V7X_VMEM_BYTES - 8 * 2 ** 20))


def _slab(c):
  return slice(c * LANES, (c + 1) * LANES)


def _head_norm(a, lo, gain):
  a2 = a * a
  s_lo = jnp.sum(jnp.where(lo, a2, 0.0), axis=-1, keepdims=True)
  s_hi = jnp.sum(jnp.where(lo, 0.0, a2), axis=-1, keepdims=True)
  inv = lax.rsqrt(jnp.where(lo, s_lo, s_hi) + HEAD_DIM * RMS_EPS)
  return (a * inv) * gain


def _in_proj_kernel(*refs, dilation, roles):
  x_refs = refs[:N_SLABS]
  ng_ref, w_ref, cg_ref, o_ref, hn_ref = refs[N_SLABS:N_SLABS + 5]
  tm = x_refs[0].shape[0]
  seg = tm // dilation
  two_pass = dilation > DEINTERLEAVE_STRIDE
  if two_pass:
    mid_ref, = refs[N_SLABS + 5:]
    inner, outer = DEINTERLEAVE_STRIDE, dilation // DEINTERLEAVE_STRIDE
    for c in range(N_SLABS):
      for b in range(inner):
        mid_ref[c, b] = x_refs[c][pl.ds(b, tm // inner, stride=inner), :]

  for r in range(dilation):
    if dilation == 1:
      xs = [ref[...] for ref in x_refs]
    elif two_pass:
      xs = [mid_ref[c, r % inner, pl.ds(r // inner, seg, stride=outer), :] for c in range(N_SLABS)]
    else:
      xs = [ref[pl.ds(r, seg, stride=dilation), :] for ref in x_refs]
    sq = functools.reduce(lambda a, b: a + b, [x * x for x in xs])
    inv = lax.rsqrt(jnp.sum(sq, axis=-1, keepdims=True) * (1.0 / D_MODEL) + RMS_EPS)
    for c in range(N_SLABS):
      hn_ref[r * seg:(r + 1) * seg, _slab(c)] = ((xs[c] * inv) * ng_ref[:, _slab(c)]).astype(BF16)
  _project(hn_ref, w_ref, cg_ref, o_ref, dilation, roles)


NORM, PLAIN, SILU = "norm", "plain", "silu"


def _project(hn_ref, w_ref, cg_ref, o_ref, dilation, roles):
  tm = hn_ref.shape[0]
  seg = tm // dilation
  lo = lax.broadcasted_iota(jnp.int32, (tm, LANES), 1) < HEAD_DIM
  for cb, role in enumerate(roles):
    acc = jnp.dot(hn_ref[...], w_ref[:, cb * BRANCH:(cb + 1) * BRANCH],
                  preferred_element_type=F32)
    for c in range(N_PAIRS):
      blk = cb * N_PAIRS + c
      a = acc[:, _slab(c)]
      if role == NORM:
        a = _head_norm(a, lo, cg_ref[:, _slab(blk)])
      elif role == SILU:
        a = jax.nn.silu(a)
      y = a.astype(BF16)
      for r in range(dilation):
        o_ref[r, blk] = y[r * seg:(r + 1) * seg]


def _in_proj(x2d, batch, norm_gain, w_bf16, col_gain, roles, dilation, tm=512):
  n, d = x2d.shape
  c = w_bf16.shape[1]
  s = n // batch
  tiles = s // tm
  seg = tm // dilation
  assert s % tm == 0 and seg % BF16_ROWS == 0 and c == BRANCH * len(roles)
  const = lambda i: (0, 0)
  block_bytes = tm * d * 4 + tm * c * 2 + (d + c) * 4
  kern = functools.partial(_in_proj_kernel, dilation=dilation, roles=roles)
  return pl.pallas_call(
      kern,
      grid=(n // tm,),
      in_specs=[pl.BlockSpec((tm, LANES), lambda i, cc=cc: (i, cc)) for cc in range(N_SLABS)] + [
          pl.BlockSpec((1, d), const),
          pl.BlockSpec((d, c), const, pipeline_mode=pl.Buffered(1)),
          pl.BlockSpec((1, c), const),
      ],
      out_specs=pl.BlockSpec((None, dilation, c // LANES, seg, LANES),
                             lambda i: (i // tiles, 0, 0, i % tiles, 0)),
      out_shape=jax.ShapeDtypeStruct((batch, dilation, c // LANES, s // dilation, LANES), BF16),
      scratch_shapes=[pltpu.VMEM((tm, d), BF16)] + (
          [pltpu.VMEM((N_SLABS, DEINTERLEAVE_STRIDE, tm // DEINTERLEAVE_STRIDE, LANES), F32)]
          if dilation > DEINTERLEAVE_STRIDE else []),
      compiler_params=pltpu.CompilerParams(
          dimension_semantics=("parallel",),
          vmem_limit_bytes=_vmem_limit(block_bytes, d * c * 2 + tm * d * 6 + 3 * tm * BRANCH * 4)),
      name=f"in_proj_d{dilation}",
  )(*([x2d] * N_SLABS), norm_gain.reshape(1, d), w_bf16, col_gain.reshape(1, c))


GROUP = 4


def _divmod_pow2(x, n):
  assert n & (n - 1) == 0
  return lax.shift_right_logical(x, n.bit_length() - 1), x & (n - 1)


def _pipeline(n_groups, stage1, stage2, stage3):
  assert n_groups % 2 == 0 and n_groups >= 2
  stage1(0, 0)
  stage1(1, 1)
  stage2(0, 0)

  def step(j, carry):
    g = 2 * j + 2
    stage1(g, 0)
    stage2(g - 1, 1)
    stage3(g - 2, 0)
    stage1(g + 1, 1)
    stage2(g, 0)
    stage3(g - 1, 1)
    return carry

  lax.fori_loop(0, (n_groups - 2) // 2, step, 0)
  stage2(n_groups - 1, 1)
  stage3(n_groups - 2, 0)
  stage3(n_groups - 1, 1)


def _attention_stages(q_ref, k_ref, v_ref, b_ref, bufs, *, tq, tk, tiles_per_pair, geometry, emit):
  lo = lax.broadcasted_iota(jnp.int32, (tq, LANES), 1) < HEAD_DIM

  def locate(g, u):
    pair, t = _divmod_pow2(jnp.asarray(g, jnp.int32) * GROUP + u, tiles_per_pair)
    return pair, geometry(t)

  def stage1(g, par):
    s_buf = bufs[par][0]
    for u in range(GROUP):
      pair, (q0, k0, var) = locate(g, u)
      q = q_ref[pair, pl.ds(q0, tq), :]
      zero = jnp.zeros_like(q)
      q2 = jnp.concatenate([jnp.where(lo, q, zero), jnp.where(lo, zero, q)], axis=0)
      k = k_ref[pair, pl.ds(k0, tk), :]
      s = lax.dot_general(q2, k, (((1,), (1,)), ((), ())), preferred_element_type=F32)
      s_buf[u] = s + b_ref[var, pair]

  def stage2(g, par):
    s_buf, p_buf, m_buf = bufs[par]
    for u in range(GROUP):
      m = jnp.max(s_buf[u], axis=-1, keepdims=True)
      p_buf[u] = jnp.exp2(s_buf[u] - m).astype(BF16)
      if m_buf is not None:
        m_buf[u] = jnp.where(lo, m[:tq], m[tq:])

  ones = jnp.ones((tk, LANES), BF16)

  def stage3(g, par):
    _, p_buf, m_buf = bufs[par]
    for u in range(GROUP):
      pair, (_, k0, _) = locate(g, u)
      v1 = jnp.concatenate([v_ref[pair, pl.ds(k0, tk), :], ones], axis=1)
      pv = jnp.dot(p_buf[u], v1, preferred_element_type=F32)
      acc = jnp.where(lo, pv[:tq, :LANES], pv[tq:, :LANES])
      l = jnp.where(lo, pv[:tq, LANES:], pv[tq:, LANES:])
      lse = None if m_buf is None else m_buf[u] + jnp.log2(l)
      emit(g, u, acc * (1.0 / l), lse)

  return stage1, stage2, stage3


def _handoff_scratch(tq, tk, with_lse):
  one = [pltpu.VMEM((GROUP, 2 * tq, tk), F32), pltpu.VMEM((GROUP, 2 * tq, tk), BF16)]
  if with_lse:
    one.append(pltpu.VMEM((GROUP, tq, LANES), F32))
  return one + one


def _split_handoff(scratch, with_lse):
  n = 3 if with_lse else 2
  bufs = []
  for par in range(2):
    b = tuple(scratch[par * n:(par + 1) * n])
    bufs.append(b if with_lse else b + (None,))
  return bufs, scratch[2 * n:]


A_TQ = 128
A_TK = 256


def _dilated_attn_kernel(q_ref, k_ref, v_ref, b_ref, o_ref, l_ref, *scratch,
                         n_pairs, seq, spread, tiles_per_store):
  bufs, rest = _split_handoff(scratch, True)
  n_tiles = seq // A_TQ
  rows = tiles_per_store * A_TQ
  seg = rows // spread

  def geometry(t):
    q0 = pl.multiple_of(t * A_TQ, A_TQ)
    k0 = pl.multiple_of(jnp.clip(q0 - A_TQ // 2, 0, seq - A_TK), A_TQ // 2)
    var = jnp.where(t == 0, 0, jnp.where(t == n_tiles - 1, 2, 1))
    return q0, k0, var

  def emit(g, u, out, lse):
    pair, t = _divmod_pow2(jnp.asarray(g, jnp.int32) * GROUP + u, n_tiles)
    if spread == 1:
      q0 = pl.multiple_of(t * A_TQ, A_TQ)
      o_ref[0, pair, pl.ds(q0, A_TQ), :] = out.astype(BF16)
      l_ref[0, pair, pl.ds(q0, A_TQ), :] = lse
      return
    so_ref, sl_ref = rest[:2]
    slot, w = divmod(u, tiles_per_store)
    so_ref[slot, w * A_TQ:(w + 1) * A_TQ, :] = out
    sl_ref[slot, w * A_TQ:(w + 1) * A_TQ, :] = lse
    if w != tiles_per_store - 1:
      return
    base = pl.multiple_of(_divmod_pow2(t, tiles_per_store)[0] * seg, seg)
    if spread <= DEINTERLEAVE_STRIDE:
      for kk in range(spread):
        o_ref[kk, pair, pl.ds(base, seg), :] = so_ref[slot, pl.ds(kk, seg, stride=spread), :].astype(BF16)
        l_ref[kk, pair, pl.ds(base, seg), :] = sl_ref[slot, pl.ds(kk, seg, stride=spread), :]
      return
    inner, outer = DEINTERLEAVE_STRIDE, spread // DEINTERLEAVE_STRIDE
    for src, mid, dst, cast in ((so_ref, rest[2], o_ref, True), (sl_ref, rest[3], l_ref, False)):
      for b in range(inner):
        mid[slot, b] = src[slot, pl.ds(b, rows // inner, stride=inner), :]
      for b in range(inner):
        for a in range(outer):
          part = mid[slot, b, pl.ds(a, seg, stride=outer), :]
          dst[a * inner + b, pair, pl.ds(base, seg), :] = part.astype(BF16) if cast else part

  stages = _attention_stages(q_ref, k_ref, v_ref, b_ref, bufs, tq=A_TQ, tk=A_TK,
                             tiles_per_pair=n_tiles, geometry=geometry, emit=emit)
  _pipeline(n_pairs * n_tiles // GROUP, *stages)


def _t5_bucket(rel):
  half = T5_BUCKETS // 2
  max_exact = half // 2
  ret = jnp.where(rel > 0, half, 0)
  n = jnp.abs(rel)
  nf = jnp.maximum(n, 1).astype(jnp.float32)
  large = max_exact + (jnp.log(nf / max_exact) / math.log(T5_MAX_DISTANCE / max_exact)
                       * (half - max_exact)).astype(jnp.int32)
  large = jnp.minimum(large, half - 1)
  return ret + jnp.where(n < max_exact, n, large)


def _stack_pairs(bias):
  h, tq, tk = bias.shape[-3:]
  return bias.reshape(bias.shape[:-3] + (h // PAIR, PAIR * tq, tk))


def _dilated_bias(t5_bias, reach):
  rel = np.arange(-reach, reach + 1)
  bands = []
  for g, (_, dilation) in enumerate(DILATED_PAIRS):
    table = t5_bias[g * N_HEADS:(g + 1) * N_HEADS]
    bands.append(table[:, _t5_bucket(jnp.asarray(rel * dilation, dtype=jnp.int32))])
  band = jnp.stack(bands).astype(F32) * LOG2E
  n_diag = A_TQ + A_TK - 1
  padded = jnp.pad(band, ((0, 0), (0, 0), (n_diag, n_diag)), constant_values=MASK_VALUE)
  diags = []
  for delta in (0, -reach, -2 * reach):
    start = n_diag + reach + delta - (A_TQ - 1)
    diags.append(padded[:, :, start:start + n_diag])
  return _stack_pairs(_toeplitz(jnp.stack(diags, axis=1)))


def _toeplitz(diag):
  lead = diag.shape[:-1]
  n_diag = diag.shape[-1]
  ext = jnp.concatenate([diag, jnp.full(lead + (1,), MASK_VALUE, diag.dtype)], axis=-1)
  rows = jnp.tile(ext, A_TQ)[..., :A_TQ * n_diag].reshape(lead + (A_TQ, n_diag))
  return rows[..., A_TQ - 1:]


def _dilated_attention(qkv, bias, group, n_pairs):
  b, d, _, seq, _ = qkv.shape
  spread = MAX_DILATION // d
  planes = seq // spread
  n_hb = N_PAIRS // n_pairs
  n_tiles = seq // A_TQ
  tiles_per_store = max(1, BF16_ROWS * spread // A_TQ)
  assert (tiles_per_store * A_TQ // spread) % BF16_ROWS == 0 and GROUP % tiles_per_store == 0
  assert n_tiles % tiles_per_store == 0 and (n_pairs * n_tiles) % (2 * GROUP) == 0
  assert n_tiles % GROUP == 0 or GROUP % n_tiles == 0

  def in_map(t):
    return lambda hb, bi, r: (bi, r, t * n_hb + hb, 0, 0)

  out_map = lambda hb, bi, r: (bi, 0, r, hb, 0, 0)
  in_blk = (None, None, n_pairs, seq, LANES)
  out_blk = (None, spread, None, n_pairs, planes, LANES)
  block_bytes = n_pairs * seq * LANES * (3 * 2 + 2 + 4) + 3 * n_pairs * PAIR * A_TQ * A_TK * 4
  scratch = _handoff_scratch(A_TQ, A_TK, True)
  if spread > 1:
    slots, rows = GROUP // tiles_per_store, tiles_per_store * A_TQ
    scratch += [pltpu.VMEM((slots, rows, LANES), F32)] * 2
  if spread > DEINTERLEAVE_STRIDE:
    assert spread % DEINTERLEAVE_STRIDE == 0
    scratch += [pltpu.VMEM((slots, DEINTERLEAVE_STRIDE, rows // DEINTERLEAVE_STRIDE, LANES), F32)] * 2
  kern = functools.partial(_dilated_attn_kernel, n_pairs=n_pairs, seq=seq, spread=spread,
                           tiles_per_store=tiles_per_store)
  o, lse = pl.pallas_call(
      kern,
      grid=(n_hb, b, d),
      in_specs=[
          pl.BlockSpec(in_blk, in_map(0)),
          pl.BlockSpec(in_blk, in_map(1)),
          pl.BlockSpec(in_blk, in_map(2)),
          pl.BlockSpec((None, 3, n_pairs, PAIR * A_TQ, A_TK), lambda hb, bi, r: (group, 0, hb, 0, 0)),
      ],
      out_specs=[pl.BlockSpec(out_blk, out_map), pl.BlockSpec(out_blk, out_map)],
      out_shape=[jax.ShapeDtypeStruct((b, spread, d, N_PAIRS, planes, LANES), BF16),
                 jax.ShapeDtypeStruct((b, spread, d, N_PAIRS, planes, LANES), F32)],
      scratch_shapes=scratch,
      compiler_params=pltpu.CompilerParams(
          dimension_semantics=("parallel", "parallel", "parallel"),
          vmem_limit_bytes=_vmem_limit(block_bytes, 12 * 2 ** 20)),
      name=f"dilated_attn_d{d}",
  )(qkv, qkv, qkv, bias)
  shape = (b, MAX_DILATION, N_PAIRS, planes, LANES)
  return o.reshape(shape), lse.reshape(shape)


NA_TK = NA_ROWS * GRID_W


def _na_attn_kernel(q_ref, k_ref, v_ref, b_ref, o_ref, *scratch, rows):
  bufs, _ = _split_handoff(scratch, False)

  def geometry(r):
    rs = jnp.clip(r - NA_ROWS // 2, 0, rows - NA_ROWS)
    return (pl.multiple_of(r * GRID_W, GRID_W), pl.multiple_of(rs * GRID_W, GRID_W),
            rs - r + NA_ROWS - 1)

  def emit(g, u, out, _):
    q0 = pl.multiple_of((jnp.asarray(g, jnp.int32) * GROUP + u) * GRID_W, GRID_W)
    o_ref[0, pl.ds(q0, GRID_W), :] = out.astype(BF16)

  stages = _attention_stages(q_ref, k_ref, v_ref, b_ref, bufs, tq=GRID_W, tk=NA_TK,
                             tiles_per_pair=rows, geometry=geometry, emit=emit)
  _pipeline(rows // GROUP, *stages)


def _na_bias(rpb):
  qc = np.arange(GRID_W)[:, None]
  kc = np.arange(GRID_W)[None, :]
  cstart = np.clip(qc - NA_COLS // 2, 0, GRID_W - NA_COLS)
  valid = (kc >= cstart) & (kc < cstart + NA_COLS)
  col_idx = np.clip(kc - qc, -(NA_COLS - 1), NA_COLS - 1) + NA_COLS - 1
  col_onehot = (col_idx[..., None] == np.arange(2 * NA_COLS - 1)).astype(np.float32)
  cols = jnp.einsum("qkc,hjc->hqjk", col_onehot, rpb.astype(F32) * LOG2E,
                    precision=lax.Precision.HIGHEST)
  cols = jnp.where(jnp.asarray(valid)[None, :, None, :], cols, MASK_VALUE)
  cols = cols.reshape(N_HEADS, GRID_W, (2 * NA_ROWS - 1) * GRID_W)
  return _stack_pairs(jnp.stack([cols[:, :, u * GRID_W:u * GRID_W + NA_TK] for u in range(NA_ROWS)]))


def _na_attention(proj, bias):
  b, _, s, _ = proj.shape
  rows = s // GRID_W
  blk = (None, 1, s, LANES)
  assert rows % (2 * GROUP) == 0
  block_bytes = s * LANES * 2 * 4 + NA_ROWS * PAIR * GRID_W * NA_TK * 4
  return pl.pallas_call(
      functools.partial(_na_attn_kernel, rows=rows),
      grid=(N_PAIRS, b),
      in_specs=[
          pl.BlockSpec(blk, lambda hb, bi: (bi, hb, 0, 0)),
          pl.BlockSpec(blk, lambda hb, bi: (bi, N_PAIRS + hb, 0, 0)),
          pl.BlockSpec(blk, lambda hb, bi: (bi, 2 * N_PAIRS + hb, 0, 0)),
          pl.BlockSpec((NA_ROWS, 1, PAIR * GRID_W, NA_TK), lambda hb, bi: (0, hb, 0, 0)),
      ],
      out_specs=pl.BlockSpec(blk, lambda hb, bi: (bi, hb, 0, 0)),
      out_shape=jax.ShapeDtypeStruct((b, N_PAIRS, s, LANES), BF16),
      scratch_shapes=_handoff_scratch(GRID_W, NA_TK, False),
      compiler_params=pltpu.CompilerParams(
          dimension_semantics=("parallel", "parallel"),
          vmem_limit_bytes=_vmem_limit(block_bytes, 12 * 2 ** 20)),
      name="na_attn",
  )(proj, proj, proj, bias)


def _merge_out_proj_kernel(*refs, n_groups, next_roles):
  o_refs = refs[:n_groups]
  l_refs = refs[n_groups:2 * n_groups]
  (g_ref, x_ref, w_ref, ng_ref, w2_ref, cg2_ref, out_ref, proj_ref,
   y_slab, y_bf, hn_ref) = refs[2 * n_groups:]
  seg = g_ref.shape[2]

  @pl.when(pl.program_id(0) == 0)
  def _():
    hn_ref[...] = jnp.zeros_like(hn_ref)

  def merge_plane(r):
    for c in range(N_PAIRS):
      lses = [ref[r, c] for ref in l_refs]
      mx = functools.reduce(jnp.maximum, lses)
      es = [jnp.exp2(l - mx) for l in lses]
      den = functools.reduce(lambda a, b: a + b, es)
      num = functools.reduce(lambda a, b: a + b,
                             [e * ref[r, c].astype(F32) for e, ref in zip(es, o_refs)])
      y = (num / den) * g_ref[r, c].astype(F32)
      y_slab[c, pl.ds(r, seg, stride=MAX_DILATION), :] = y

  _project(hn_ref, w2_ref, cg2_ref, proj_ref, 1, next_roles)
  for r in range(MAX_DILATION):
    merge_plane(r)
  for c in range(N_PAIRS):
    y_bf[:, _slab(c)] = y_slab[c].astype(BF16)
  xn = x_ref[...] + jnp.dot(y_bf[...], w_ref[...], preferred_element_type=F32)
  out_ref[...] = xn
  inv = lax.rsqrt(jnp.mean(xn * xn, axis=-1, keepdims=True) + RMS_EPS)
  hn_ref[...] = ((xn * inv) * ng_ref[...]).astype(BF16)


def _merge_out_proj(outs, lses, gated, gate_block, x2d, w_bf16,
                    next_norm_gain, next_w_bf16, next_col_gain, next_roles, tm=256):
  n, d = x2d.shape
  b, _, _, planes, _ = outs[0].shape
  s = planes * MAX_DILATION
  c2 = next_w_bf16.shape[1]
  seg = tm // MAX_DILATION
  tiles = planes // seg
  n_groups = len(outs)
  assert seg % BF16_ROWS == 0 and planes % seg == 0 and c2 == BRANCH * len(next_roles)
  res_blk = (None, MAX_DILATION, N_PAIRS, seg, LANES)
  n_steps = n // tm
  merged = lambda i: jnp.minimum(i, n_steps - 1)
  projected = lambda i: jnp.maximum(i - 1, 0)
  res_map = lambda i: (merged(i) // tiles, 0, 0, merged(i) % tiles, 0)
  const = lambda i: (0, 0)
  block_bytes = tm * BRANCH * (2 * n_groups + 4 * n_groups + 2) + 2 * tm * d * 4 + tm * c2 * 2
  return pl.pallas_call(
      functools.partial(_merge_out_proj_kernel, n_groups=n_groups, next_roles=next_roles),
      grid=(n_steps + 1,),
      in_specs=[pl.BlockSpec(res_blk, res_map)] * (2 * n_groups) + [
          pl.BlockSpec(res_blk, lambda i: (merged(i) // tiles, 0, gate_block, merged(i) % tiles, 0)),
          pl.BlockSpec((tm, d), lambda i: (merged(i), 0)),
          pl.BlockSpec((BRANCH, d), const, pipeline_mode=pl.Buffered(1)),
          pl.BlockSpec((1, d), const),
          pl.BlockSpec((d, c2), const, pipeline_mode=pl.Buffered(1)),
          pl.BlockSpec((1, c2), const),
      ],
      out_specs=[pl.BlockSpec((tm, d), lambda i: (merged(i), 0)),
                 pl.BlockSpec((None, 1, c2 // LANES, tm, LANES),
                              lambda i: (projected(i) // tiles, 0, 0, projected(i) % tiles, 0))],
      out_shape=[jax.ShapeDtypeStruct((n, d), F32),
                 jax.ShapeDtypeStruct((b, 1, c2 // LANES, s, LANES), BF16)],
      scratch_shapes=[pltpu.VMEM((N_PAIRS, tm, LANES), F32), pltpu.VMEM((tm, BRANCH), BF16),
                      pltpu.VMEM((tm, d), BF16)],
      compiler_params=pltpu.CompilerParams(
          dimension_semantics=("arbitrary",),
          vmem_limit_bytes=_vmem_limit(block_bytes, (BRANCH * d + d * c2) * 2 + tm * BRANCH * 8
                                       + 3 * tm * BRANCH * 4 + 4 * 2 ** 20)),
      name="merge_out_in_proj",
  )(*outs, *lses, gated, x2d, w_bf16, next_norm_gain.reshape(1, d), next_w_bf16,
    next_col_gain.reshape(1, c2))


def _out_proj_kernel(o_ref, g_ref, x_ref, w_ref, out_ref, y_bf):
  for c in range(N_PAIRS):
    y = o_ref[c].astype(F32) * g_ref[c].astype(F32)
    y_bf[:, _slab(c)] = y.astype(BF16)
  out_ref[...] = x_ref[...] + jnp.dot(y_bf[...], w_ref[...], preferred_element_type=F32)


def _out_proj(o, proj, gate_block, x2d, w_bf16, tm=1024):
  n, d = x2d.shape
  b, _, s, _ = o.shape
  tiles = s // tm
  blk = (None, N_PAIRS, tm, LANES)
  block_bytes = tm * BRANCH * 4 + 2 * tm * d * 4
  return pl.pallas_call(
      _out_proj_kernel,
      grid=(n // tm,),
      in_specs=[
          pl.BlockSpec(blk, lambda i: (i // tiles, 0, i % tiles, 0)),
          pl.BlockSpec(blk, lambda i: (i // tiles, gate_block, i % tiles, 0)),
          pl.BlockSpec((tm, d), lambda i: (i, 0)),
          pl.BlockSpec((BRANCH, d), lambda i: (0, 0), pipeline_mode=pl.Buffered(1)),
      ],
      out_specs=pl.BlockSpec((tm, d), lambda i: (i, 0)),
      out_shape=jax.ShapeDtypeStruct((n, d), F32),
      scratch_shapes=[pltpu.VMEM((tm, BRANCH), BF16)],
      compiler_params=pltpu.CompilerParams(
          dimension_semantics=("parallel",),
          vmem_limit_bytes=_vmem_limit(block_bytes, BRANCH * d * 2 + tm * BRANCH * 2 + 2 * tm * d * 4)),
      name="out_proj",
  )(o, proj, x2d, w_bf16)


QKV_ROLES = (NORM, NORM, PLAIN)


def _col_gain(q_gain, k_gain):
  q = jnp.tile(q_gain.astype(F32) * LOG2E, N_HEADS)
  k = jnp.tile(k_gain.astype(F32) * HEAD_DIM ** 0.5, N_HEADS)
  return jnp.concatenate([q, k, jnp.ones((BRANCH,), F32)])


def _layer_a(x, norm_gain, w_in, w_out, q_gain, k_gain, t5_bias, next_in_proj):
  b, s, d = x.shape
  x2d = x.reshape(b * s, d)
  ones = jnp.ones((BRANCH,), F32)
  w_in = w_in.astype(BF16)
  reach = A_TQ // 2
  assert all((window // 2) // dilation == reach for window, dilation in DILATED_PAIRS)
  bias = _dilated_bias(t5_bias, reach)
  outs, lses, gated = [], [], None
  for g, (window, dilation) in enumerate(DILATED_PAIRS):
    seq = s // dilation
    assert seq >= A_TK and MAX_DILATION % dilation == 0
    last = g == N_GROUPS - 1
    cols = slice(g * 3 * BRANCH, (g + 1) * 3 * BRANCH + (BRANCH if last else 0))
    gain = _col_gain(q_gain[g], k_gain[g])
    roles = QKV_ROLES + ((SILU,) if last else ())
    qkv = _in_proj(x2d, b, norm_gain, w_in[:, cols],
                   jnp.concatenate([gain, ones]) if last else gain, roles, dilation)
    n_pairs = max(1, min(N_PAIRS, 4096 // seq))
    o, lse = _dilated_attention(qkv, bias, g, n_pairs)
    outs.append(o)
    lses.append(lse)
    if last:
      assert dilation == MAX_DILATION
      gated = qkv
  y, proj = _merge_out_proj(outs, lses, gated, 3, x2d, w_out.astype(BF16), *next_in_proj)
  return y.reshape(b, s, d), proj


B_ROLES = QKV_ROLES + (SILU,)


def _b_in_proj_args(norm_gain, w_in, q_gain, k_gain):
  gains = jnp.concatenate([_col_gain(q_gain, k_gain), jnp.ones((BRANCH,), F32)])
  return norm_gain, w_in.astype(BF16), gains, B_ROLES


def _layer_b(x, proj, w_out, rpb):
  b, s, d = x.shape
  proj = proj.reshape(b, 4 * N_PAIRS, s, LANES)
  o = _na_attention(proj, _na_bias(rpb))
  y = _out_proj(o, proj, 3, x.reshape(b * s, d), w_out.astype(BF16))
  return y.reshape(b, s, d)


def kernel(x, norm_gain, a_w_in, a_w_out, a_q_gain, a_k_gain, t5_bias,
           b_w_in, b_w_out, b_q_gain, b_k_gain, b_rpb):
  assert norm_gain.shape[0] == 2 and a_w_in.shape[0] == 1 and b_w_in.shape[0] == 1
  x, proj = _layer_a(x, norm_gain[0], a_w_in[0], a_w_out[0], a_q_gain[0], a_k_gain[0], t5_bias,
                     _b_in_proj_args(norm_gain[1], b_w_in[0], b_q_gain[0], b_k_gain[0]))
  return _layer_b(x, proj, b_w_out[0], b_rpb[0])
```

```python
import functools
import math

import numpy as np
import jax
import jax.numpy as jnp
from jax import lax
from jax.experimental import pallas as pl
from jax.experimental.pallas import tpu as pltpu

D_MODEL = 1024
HEAD_DIM = 64
N_HEADS = D_MODEL // HEAD_DIM
BRANCH = N_HEADS * HEAD_DIM
DILATED_PAIRS = ((128, 1), (512, 4), (2048, 16))
N_GROUPS = len(DILATED_PAIRS)
MAX_DILATION = max(d for _, d in DILATED_PAIRS)
T5_BUCKETS = 32
T5_MAX_DISTANCE = 1024
GRID_W = 64
NA_ROWS = 8
NA_COLS = 16
RMS_EPS = 1e-6
MASK_VALUE = -1e30
LOG2E = math.log2(math.e)

LANES = 128
BF16_ROWS = 16
V7X_VMEM_BYTES = 64 * 2 ** 20
PAIR = LANES // HEAD_DIM
N_SLABS = D_MODEL // LANES
N_PAIRS = BRANCH // LANES
DEINTERLEAVE_STRIDE = 4

F32 = jnp.float32
BF16 = jnp.bfloat16


def _vmem_limit(block_bytes, extra_bytes):
  need = 2 * block_bytes + extra_bytes
  return int(min(need + need // 4, V7X_VMEM_BYTES - 8 * 2 ** 20))


def _slab(c):
  return slice(c * LANES, (c + 1) * LANES)


def _head_norm(a, lo, gain):
  a2 = a * a
  s_lo = jnp.sum(jnp.where(lo, a2, 0.0), axis=-1, keepdims=True)
  s_hi = jnp.sum(jnp.where(lo, 0.0, a2), axis=-1, keepdims=True)
  inv = lax.rsqrt(jnp.where(lo, s_lo, s_hi) + HEAD_DIM * RMS_EPS)
  return (a * inv) * gain


def _in_proj_kernel(*refs, dilation, roles, n_w):
  x_refs = refs[:N_SLABS]
  ng_ref = refs[N_SLABS]
  w_pieces = refs[N_SLABS + 1:N_SLABS + 1 + n_w]
  cg_ref, o_ref, hn_ref, w_ref = refs[N_SLABS + 1 + n_w:N_SLABS + 5 + n_w]
  tm = x_refs[0].shape[0]
  seg = tm // dilation

  @pl.when(pl.program_id(0) == 0)
  def _():
    col = 0
    for piece in w_pieces:
      w_ref[:, col:col + piece.shape[1]] = piece[...].astype(BF16)
      col += piece.shape[1]

  two_pass = dilation > DEINTERLEAVE_STRIDE
  if two_pass:
    mid_ref, = refs[N_SLABS + 5 + n_w:]
    inner, outer = DEINTERLEAVE_STRIDE, dilation // DEINTERLEAVE_STRIDE
    for c in range(N_SLABS):
      for b in range(inner):
        mid_ref[c, b] = x_refs[c][pl.ds(b, tm // inner, stride=inner), :]

  for r in range(dilation):
    if dilation == 1:
      xs = [ref[...] for ref in x_refs]
    elif two_pass:
      xs = [mid_ref[c, r % inner, pl.ds(r // inner, seg, stride=outer), :] for c in range(N_SLABS)]
    else:
      xs = [ref[pl.ds(r, seg, stride=dilation), :] for ref in x_refs]
    sq = functools.reduce(lambda a, b: a + b, [x * x for x in xs])
    inv = lax.rsqrt(jnp.sum(sq, axis=-1, keepdims=True) * (1.0 / D_MODEL) + RMS_EPS)
    for c in range(N_SLABS):
      hn_ref[r * seg:(r + 1) * seg, _slab(c)] = ((xs[c] * inv) * ng_ref[:, _slab(c)]).astype(BF16)
  _project(hn_ref, w_ref, cg_ref, o_ref, dilation, roles)


NORM, PLAIN, SILU = "norm", "plain", "silu"


def _project(hn_ref, w_ref, cg_ref, o_ref, dilation, roles):
  tm = hn_ref.shape[0]
  seg = tm // dilation
  lo = lax.broadcasted_iota(jnp.int32, (tm, LANES), 1) < HEAD_DIM
  for cb, role in enumerate(roles):
    acc = jnp.dot(hn_ref[...], w_ref[:, cb * BRANCH:(cb + 1) * BRANCH],
                  preferred_element_type=F32)
    for c in range(N_PAIRS):
      blk = cb * N_PAIRS + c
      a = acc[:, _slab(c)]
      if role == NORM:
        a = _head_norm(a, lo, cg_ref[:, _slab(blk)])
      elif role == SILU:
        a = jax.nn.silu(a)
      y = a.astype(BF16)
      for r in range(dilation):
        o_ref[r, blk] = y[r * seg:(r + 1) * seg]


def _in_proj(x2d, batch, norm_gain, w, w_cols, col_gain, roles, dilation, tm=512):
  n, d = x2d.shape
  c = sum(width for _, width in w_cols)
  s = n // batch
  tiles = s // tm
  seg = tm // dilation
  assert s % tm == 0 and seg % BF16_ROWS == 0 and c == BRANCH * len(roles)
  assert all(first % width == 0 for first, width in w_cols)
  const = lambda i: (0, 0)
  block_bytes = tm * d * 4 + tm * c * 2 + (d + c) * 4
  kern = functools.partial(_in_proj_kernel, dilation=dilation, roles=roles, n_w=len(w_cols))
  return pl.pallas_call(
      kern,
      grid=(n // tm,),
      in_specs=[pl.BlockSpec((tm, LANES), lambda i, cc=cc: (i, cc)) for cc in range(N_SLABS)] + [
          pl.BlockSpec((1, d), const)] + [
          pl.BlockSpec((d, width), lambda i, blk=first // width: (0, blk), pipeline_mode=pl.Buffered(1))
          for first, width in w_cols] + [
          pl.BlockSpec((1, c), const),
      ],
      out_specs=pl.BlockSpec((None, dilation, c // LANES, seg, LANES),
                             lambda i: (i // tiles, 0, 0, i % tiles, 0)),
      out_shape=jax.ShapeDtypeStruct((batch, dilation, c // LANES, s // dilation, LANES), BF16),
      scratch_shapes=[pltpu.VMEM((tm, d), BF16), pltpu.VMEM((d, c), BF16)] + (
          [pltpu.VMEM((N_SLABS, DEINTERLEAVE_STRIDE, tm // DEINTERLEAVE_STRIDE, LANES), F32)]
          if dilation > DEINTERLEAVE_STRIDE else []),
      compiler_params=pltpu.CompilerParams(
          dimension_semantics=("arbitrary",),
          vmem_limit_bytes=_vmem_limit(block_bytes, d * c * 6 + tm * d * 6 + 3 * tm * BRANCH * 4)),
      name=f"in_proj_d{dilation}",
  )(*([x2d] * N_SLABS), norm_gain.reshape(1, d), *([w] * len(w_cols)), col_gain.reshape(1, c))


GROUP = 4


def _divmod_pow2(x, n):
  assert n & (n - 1) == 0
  return lax.shift_right_logical(x, n.bit_length() - 1), x & (n - 1)


def _pipeline(n_groups, stage1, stage2, stage3):
  assert n_groups % 2 == 0 and n_groups >= 2
  stage1(0, 0)
  stage1(1, 1)
  stage2(0, 0)

  def step(j, carry):
    g = 2 * j + 2
    stage1(g, 0)
    stage2(g - 1, 1)
    stage3(g - 2, 0)
    stage1(g + 1, 1)
    stage2(g, 0)
    stage3(g - 1, 1)
    return carry

  lax.fori_loop(0, (n_groups - 2) // 2, step, 0)
  stage2(n_groups - 1, 1)
  stage3(n_groups - 2, 0)
  stage3(n_groups - 1, 1)


def _attention_stages(q_ref, k_ref, v_ref, b_ref, bufs, *, tq, tk, tiles_per_pair, geometry, emit):
  lo = lax.broadcasted_iota(jnp.int32, (tq, LANES), 1) < HEAD_DIM

  def locate(g, u):
    pair, t = _divmod_pow2(jnp.asarray(g, jnp.int32) * GROUP + u, tiles_per_pair)
    return pair, geometry(t)

  def stage1(g, par):
    s_buf = bufs[par][0]
    for u in range(GROUP):
      pair, (q0, k0, var) = locate(g, u)
      q = q_ref[pair, pl.ds(q0, tq), :]
      zero = jnp.zeros_like(q)
      q2 = jnp.concatenate([jnp.where(lo, q, zero), jnp.where(lo, zero, q)], axis=0)
      k = k_ref[pair, pl.ds(k0, tk), :]
      s = lax.dot_general(q2, k, (((1,), (1,)), ((), ())), preferred_element_type=F32)
      s_buf[u] = s + b_ref[var, pair]

  def stage2(g, par):
    s_buf, p_buf, m_buf = bufs[par]
    for u in range(GROUP):
      m = jnp.max(s_buf[u], axis=-1, keepdims=True)
      p_buf[u] = jnp.exp2(s_buf[u] - m).astype(BF16)
      if m_buf is not None:
        m_buf[u] = jnp.where(lo, m[:tq], m[tq:])

  ones = jnp.ones((tk, LANES), BF16)

  def stage3(g, par):
    _, p_buf, m_buf = bufs[par]
    for u in range(GROUP):
      pair, (_, k0, _) = locate(g, u)
      v1 = jnp.concatenate([v_ref[pair, pl.ds(k0, tk), :], ones], axis=1)
      pv = jnp.dot(p_buf[u], v1, preferred_element_type=F32)
      acc = jnp.where(lo, pv[:tq, :LANES], pv[tq:, :LANES])
      l = jnp.where(lo, pv[:tq, LANES:], pv[tq:, LANES:])
      lse = None if m_buf is None else m_buf[u] + jnp.log2(l)
      emit(g, u, acc * (1.0 / l), lse)

  return stage1, stage2, stage3


def _handoff_scratch(tq, tk, with_lse):
  one = [pltpu.VMEM((GROUP, 2 * tq, tk), F32), pltpu.VMEM((GROUP, 2 * tq, tk), BF16)]
  if with_lse:
    one.append(pltpu.VMEM((GROUP, tq, LANES), F32))
  return one + one


def _split_handoff(scratch, with_lse):
  n = 3 if with_lse else 2
  bufs = []
  for par in range(2):
    b = tuple(scratch[par * n:(par + 1) * n])
    bufs.append(b if with_lse else b + (None,))
  return bufs, scratch[2 * n:]


A_TQ = 128
A_TK = 256


def _dilated_attn_kernel(q_ref, k_ref, v_ref, b_ref, o_ref, l_ref, *scratch,
                         n_pairs, seq, spread, tiles_per_store):
  bufs, rest = _split_handoff(scratch, True)
  n_tiles = seq // A_TQ
  rows = tiles_per_store * A_TQ
  seg = rows // spread

  def geometry(t):
    q0 = pl.multiple_of(t * A_TQ, A_TQ)
    k0 = pl.multiple_of(jnp.clip(q0 - A_TQ // 2, 0, seq - A_TK), A_TQ // 2)
    var = jnp.where(t == 0, 0, jnp.where(t == n_tiles - 1, 2, 1))
    return q0, k0, var

  def emit(g, u, out, lse):
    pair, t = _divmod_pow2(jnp.asarray(g, jnp.int32) * GROUP + u, n_tiles)
    if spread == 1:
      q0 = pl.multiple_of(t * A_TQ, A_TQ)
      o_ref[0, pair, pl.ds(q0, A_TQ), :] = out.astype(BF16)
      l_ref[0, pair, pl.ds(q0, A_TQ), :] = lse
      return
    so_ref, sl_ref = rest[:2]
    slot, w = divmod(u, tiles_per_store)
    so_ref[slot, w * A_TQ:(w + 1) * A_TQ, :] = out
    sl_ref[slot, w * A_TQ:(w + 1) * A_TQ, :] = lse
    if w != tiles_per_store - 1:
      return
    base = pl.multiple_of(_divmod_pow2(t, tiles_per_store)[0] * seg, seg)
    if spread <= DEINTERLEAVE_STRIDE:
      for kk in range(spread):
        o_ref[kk, pair, pl.ds(base, seg), :] = so_ref[slot, pl.ds(kk, seg, stride=spread), :].astype(BF16)
        l_ref[kk, pair, pl.ds(base, seg), :] = sl_ref[slot, pl.ds(kk, seg, stride=spread), :]
      return
    inner, outer = DEINTERLEAVE_STRIDE, spread // DEINTERLEAVE_STRIDE
    for src, mid, dst, cast in ((so_ref, rest[2], o_ref, True), (sl_ref, rest[3], l_ref, False)):
      for b in range(inner):
        mid[slot, b] = src[slot, pl.ds(b, rows // inner, stride=inner), :]
      for b in range(inner):
        for a in range(outer):
          part = mid[slot, b, pl.ds(a, seg, stride=outer), :]
          dst[a * inner + b, pair, pl.ds(base, seg), :] = part.astype(BF16) if cast else part

  stages = _attention_stages(q_ref, k_ref, v_ref, b_ref, bufs, tq=A_TQ, tk=A_TK,
                             tiles_per_pair=n_tiles, geometry=geometry, emit=emit)
  _pipeline(n_pairs * n_tiles // GROUP, *stages)


def _t5_bucket(rel):
  half = T5_BUCKETS // 2
  max_exact = half // 2
  ret = jnp.where(rel > 0, half, 0)
  n = jnp.abs(rel)
  nf = jnp.maximum(n, 1).astype(jnp.float32)
  large = max_exact + (jnp.log(nf / max_exact) / math.log(T5_MAX_DISTANCE / max_exact)
                       * (half - max_exact)).astype(jnp.int32)
  large = jnp.minimum(large, half - 1)
  return ret + jnp.where(n < max_exact, n, large)


def _stack_pairs(bias):
  h, tq, tk = bias.shape[-3:]
  return bias.reshape(bias.shape[:-3] + (h // PAIR, PAIR * tq, tk))


def _dilated_bias(t5_bias, reach):
  rel = np.arange(-reach, reach + 1)
  bands = []
  for g, (_, dilation) in enumerate(DILATED_PAIRS):
    table = t5_bias[g * N_HEADS:(g + 1) * N_HEADS]
    bands.append(table[:, _t5_bucket(jnp.asarray(rel * dilation, dtype=jnp.int32))])
  band = jnp.stack(bands).astype(F32) * LOG2E
  n_slots = 2 * reach + 2
  band = jnp.pad(band, ((0, 0), (0, 0), (0, 1)), constant_values=MASK_VALUE)
  a = np.arange(A_TQ)[None, :, None]
  jj = np.arange(A_TK)[None, None, :]
  delta = np.array([0, -reach, -2 * reach])[:, None, None]
  off = delta + jj - a
  slot = np.where(np.abs(off) <= reach, off + reach, n_slots - 1)
  onehot = (jnp.asarray(slot, jnp.int32)[..., None] == jnp.arange(n_slots, dtype=jnp.int32)).astype(F32)
  bias = jnp.einsum("vajr,ghr->gvhaj", onehot, band, precision=lax.Precision.HIGHEST)
  return _stack_pairs(bias)


def _dilated_attention(qkv, bias, group, n_pairs):
  b, d, _, seq, _ = qkv.shape
  spread = MAX_DILATION // d
  planes = seq // spread
  n_hb = N_PAIRS // n_pairs
  n_tiles = seq // A_TQ
  tiles_per_store = max(1, BF16_ROWS * spread // A_TQ)
  assert (tiles_per_store * A_TQ // spread) % BF16_ROWS == 0 and GROUP % tiles_per_store == 0
  assert n_tiles % tiles_per_store == 0 and (n_pairs * n_tiles) % (2 * GROUP) == 0
  assert n_tiles % GROUP == 0 or GROUP % n_tiles == 0

  def in_map(t):
    return lambda hb, bi, r: (bi, r, t * n_hb + hb, 0, 0)

  out_map = lambda hb, bi, r: (bi, 0, r, hb, 0, 0)
  in_blk = (None, None, n_pairs, seq, LANES)
  out_blk = (None, spread, None, n_pairs, planes, LANES)
  block_bytes = n_pairs * seq * LANES * (3 * 2 + 2 + 4) + 3 * n_pairs * PAIR * A_TQ * A_TK * 4
  scratch = _handoff_scratch(A_TQ, A_TK, True)
  if spread > 1:
    slots, rows = GROUP // tiles_per_store, tiles_per_store * A_TQ
    scratch += [pltpu.VMEM((slots, rows, LANES), F32)] * 2
  if spread > DEINTERLEAVE_STRIDE:
    assert spread % DEINTERLEAVE_STRIDE == 0
    scratch += [pltpu.VMEM((slots, DEINTERLEAVE_STRIDE, rows // DEINTERLEAVE_STRIDE, LANES), F32)] * 2
  kern = functools.partial(_dilated_attn_kernel, n_pairs=n_pairs, seq=seq, spread=spread,
                           tiles_per_store=tiles_per_store)
  o, lse = pl.pallas_call(
      kern,
      grid=(n_hb, b, d),
      in_specs=[
          pl.BlockSpec(in_blk, in_map(0)),
          pl.BlockSpec(in_blk, in_map(1)),
          pl.BlockSpec(in_blk, in_map(2)),
          pl.BlockSpec((None, 3, n_pairs, PAIR * A_TQ, A_TK), lambda hb, bi, r: (group, 0, hb, 0, 0)),
      ],
      out_specs=[pl.BlockSpec(out_blk, out_map), pl.BlockSpec(out_blk, out_map)],
      out_shape=[jax.ShapeDtypeStruct((b, spread, d, N_PAIRS, planes, LANES), BF16),
                 jax.ShapeDtypeStruct((b, spread, d, N_PAIRS, planes, LANES), F32)],
      scratch_shapes=scratch,
      compiler_params=pltpu.CompilerParams(
          dimension_semantics=("parallel", "parallel", "parallel"),
          vmem_limit_bytes=_vmem_limit(block_bytes, 12 * 2 ** 20)),
      name=f"dilated_attn_d{d}",
  )(qkv, qkv, qkv, bias)
  shape = (b, MAX_DILATION, N_PAIRS, planes, LANES)
  return o.reshape(shape), lse.reshape(shape)


NA_TK = NA_ROWS * GRID_W


def _na_attn_kernel(q_ref, k_ref, v_ref, b_ref, o_ref, *scratch, rows):
  bufs, _ = _split_handoff(scratch, False)

  def geometry(r):
    rs = jnp.clip(r - NA_ROWS // 2, 0, rows - NA_ROWS)
    return (pl.multiple_of(r * GRID_W, GRID_W), pl.multiple_of(rs * GRID_W, GRID_W),
            rs - r + NA_ROWS - 1)

  def emit(g, u, out, _):
    q0 = pl.multiple_of((jnp.asarray(g, jnp.int32) * GROUP + u) * GRID_W, GRID_W)
    o_ref[0, pl.ds(q0, GRID_W), :] = out.astype(BF16)

  stages = _attention_stages(q_ref, k_ref, v_ref, b_ref, bufs, tq=GRID_W, tk=NA_TK,
                             tiles_per_pair=rows, geometry=geometry, emit=emit)
  _pipeline(rows // GROUP, *stages)


def _na_bias(rpb):
  qc = np.arange(GRID_W)[:, None]
  kc = np.arange(GRID_W)[None, :]
  cstart = np.clip(qc - NA_COLS // 2, 0, GRID_W - NA_COLS)
  valid = (kc >= cstart) & (kc < cstart + NA_COLS)
  col_idx = np.clip(kc - qc, -(NA_COLS - 1), NA_COLS - 1) + NA_COLS - 1
  col_onehot = (col_idx[..., None] == np.arange(2 * NA_COLS - 1)).astype(np.float32)
  cols = jnp.einsum("qkc,hjc->hqjk", col_onehot, rpb.astype(F32) * LOG2E,
                    precision=lax.Precision.HIGHEST)
  cols = jnp.where(jnp.asarray(valid)[None, :, None, :], cols, MASK_VALUE)
  cols = cols.reshape(N_HEADS, GRID_W, (2 * NA_ROWS - 1) * GRID_W)
  return _stack_pairs(jnp.stack([cols[:, :, u * GRID_W:u * GRID_W + NA_TK] for u in range(NA_ROWS)]))


def _na_attention(proj, bias):
  b, _, s, _ = proj.shape
  rows = s // GRID_W
  blk = (None, 1, s, LANES)
  assert rows % (2 * GROUP) == 0
  block_bytes = s * LANES * 2 * 4 + NA_ROWS * PAIR * GRID_W * NA_TK * 4
  return pl.pallas_call(
      functools.partial(_na_attn_kernel, rows=rows),
      grid=(N_PAIRS, b),
      in_specs=[
          pl.BlockSpec(blk, lambda hb, bi: (bi, hb, 0, 0)),
          pl.BlockSpec(blk, lambda hb, bi: (bi, N_PAIRS + hb, 0, 0)),
          pl.BlockSpec(blk, lambda hb, bi: (bi, 2 * N_PAIRS + hb, 0, 0)),
          pl.BlockSpec((NA_ROWS, 1, PAIR * GRID_W, NA_TK), lambda hb, bi: (0, hb, 0, 0)),
      ],
      out_specs=pl.BlockSpec(blk, lambda hb, bi: (bi, hb, 0, 0)),
      out_shape=jax.ShapeDtypeStruct((b, N_PAIRS, s, LANES), BF16),
      scratch_shapes=_handoff_scratch(GRID_W, NA_TK, False),
      compiler_params=pltpu.CompilerParams(
          dimension_semantics=("parallel", "parallel"),
          vmem_limit_bytes=_vmem_limit(block_bytes, 12 * 2 ** 20)),
      name="na_attn",
  )(proj, proj, proj, bias)


def _merge_out_proj_kernel(*refs, n_groups, next_roles):
  o_refs = refs[:n_groups]
  l_refs = refs[n_groups:2 * n_groups]
  (g_ref, x_ref, w_ref, ng_ref, w2_ref, cg2_ref, out_ref, proj_ref,
   y_slab, y_bf, hn_ref) = refs[2 * n_groups:]
  seg = g_ref.shape[2]

  @pl.when(pl.program_id(0) == 0)
  def _():
    hn_ref[...] = jnp.zeros_like(hn_ref)

  def merge_plane(r):
    for c in range(N_PAIRS):
      lses = [ref[r, c] for ref in l_refs]
      mx = functools.reduce(jnp.maximum, lses)
      es = [jnp.exp2(l - mx) for l in lses]
      den = functools.reduce(lambda a, b: a + b, es)
      num = functools.reduce(lambda a, b: a + b,
                             [e * ref[r, c].astype(F32) for e, ref in zip(es, o_refs)])
      y = (num / den) * g_ref[r, c].astype(F32)
      y_slab[c, pl.ds(r, seg, stride=MAX_DILATION), :] = y

  _project(hn_ref, w2_ref, cg2_ref, proj_ref, 1, next_roles)
  for r in range(MAX_DILATION):
    merge_plane(r)
  for c in range(N_PAIRS):
    y_bf[:, _slab(c)] = y_slab[c].astype(BF16)
  xn = x_ref[...] + jnp.dot(y_bf[...], w_ref[...], preferred_element_type=F32)
  out_ref[...] = xn
  inv = lax.rsqrt(jnp.mean(xn * xn, axis=-1, keepdims=True) + RMS_EPS)
  hn_ref[...] = ((xn * inv) * ng_ref[...]).astype(BF16)


def _merge_out_proj(outs, lses, gated, gate_block, x2d, w_bf16,
                    next_norm_gain, next_w_bf16, next_col_gain, next_roles, tm=256):
  n, d = x2d.shape
  b, _, _, planes, _ = outs[0].shape
  s = planes * MAX_DILATION
  c2 = next_w_bf16.shape[1]
  seg = tm // MAX_DILATION
  tiles = planes // seg
  n_groups = len(outs)
  assert seg % BF16_ROWS == 0 and planes % seg == 0 and c2 == BRANCH * len(next_roles)
  res_blk = (None, MAX_DILATION, N_PAIRS, seg, LANES)
  n_steps = n // tm
  merged = lambda i: jnp.minimum(i, n_steps - 1)
  projected = lambda i: jnp.maximum(i - 1, 0)
  res_map = lambda i: (merged(i) // tiles, 0, 0, merged(i) % tiles, 0)
  const = lambda i: (0, 0)
  block_bytes = tm * BRANCH * (2 * n_groups + 4 * n_groups + 2) + 2 * tm * d * 4 + tm * c2 * 2
  return pl.pallas_call(
      functools.partial(_merge_out_proj_kernel, n_groups=n_groups, next_roles=next_roles),
      grid=(n_steps + 1,),
      in_specs=[pl.BlockSpec(res_blk, res_map)] * (2 * n_groups) + [
          pl.BlockSpec(res_blk, lambda i: (merged(i) // tiles, 0, gate_block, merged(i) % tiles, 0)),
          pl.BlockSpec((tm, d), lambda i: (merged(i), 0)),
          pl.BlockSpec((BRANCH, d), const, pipeline_mode=pl.Buffered(1)),
          pl.BlockSpec((1, d), const),
          pl.BlockSpec((d, c2), const, pipeline_mode=pl.Buffered(1)),
          pl.BlockSpec((1, c2), const),
      ],
      out_specs=[pl.BlockSpec((tm, d), lambda i: (merged(i), 0)),
                 pl.BlockSpec((None, 1, c2 // LANES, tm, LANES),
                              lambda i: (projected(i) // tiles, 0, 0, projected(i) % tiles, 0))],
      out_shape=[jax.ShapeDtypeStruct((n, d), F32),
                 jax.ShapeDtypeStruct((b, 1, c2 // LANES, s, LANES), BF16)],
      scratch_shapes=[pltpu.VMEM((N_PAIRS, tm, LANES), F32), pltpu.VMEM((tm, BRANCH), BF16),
                      pltpu.VMEM((tm, d), BF16)],
      compiler_params=pltpu.CompilerParams(
          dimension_semantics=("arbitrary",),
          vmem_limit_bytes=_vmem_limit(block_bytes, (BRANCH * d + d * c2) * 2 + tm * BRANCH * 8
                                       + 3 * tm * BRANCH * 4 + 4 * 2 ** 20)),
      name="merge_out_in_proj",
  )(*outs, *lses, gated, x2d, w_bf16, next_norm_gain.reshape(1, d), next_w_bf16,
    next_col_gain.reshape(1, c2))


def _out_proj_kernel(o_ref, g_ref, x_ref, w_ref, out_ref, y_bf):
  for c in range(N_PAIRS):
    y = o_ref[c].astype(F32) * g_ref[c].astype(F32)
    y_bf[:, _slab(c)] = y.astype(BF16)
  out_ref[...] = x_ref[...] + jnp.dot(y_bf[...], w_ref[...], preferred_element_type=F32)


def _out_proj(o, proj, gate_block, x2d, w_bf16, tm=1024):
  n, d = x2d.shape
  b, _, s, _ = o.shape
  tiles = s // tm
  blk = (None, N_PAIRS, tm, LANES)
  block_bytes = tm * BRANCH * 4 + 2 * tm * d * 4
  return pl.pallas_call(
      _out_proj_kernel,
      grid=(n // tm,),
      in_specs=[
          pl.BlockSpec(blk, lambda i: (i // tiles, 0, i % tiles, 0)),
          pl.BlockSpec(blk, lambda i: (i // tiles, gate_block, i % tiles, 0)),
          pl.BlockSpec((tm, d), lambda i: (i, 0)),
          pl.BlockSpec((BRANCH, d), lambda i: (0, 0), pipeline_mode=pl.Buffered(1)),
      ],
      out_specs=pl.BlockSpec((tm, d), lambda i: (i, 0)),
      out_shape=jax.ShapeDtypeStruct((n, d), F32),
      scratch_shapes=[pltpu.VMEM((tm, BRANCH), BF16)],
      compiler_params=pltpu.CompilerParams(
          dimension_semantics=("parallel",),
          vmem_limit_bytes=_vmem_limit(block_bytes, BRANCH * d * 2 + tm * BRANCH * 2 + 2 * tm * d * 4)),
      name="out_proj",
  )(o, proj, x2d, w_bf16)


QKV_ROLES = (NORM, NORM, PLAIN)


def _col_gain(q_gain, k_gain):
  q = jnp.tile(q_gain.astype(F32) * LOG2E, N_HEADS)
  k = jnp.tile(k_gain.astype(F32) * HEAD_DIM ** 0.5, N_HEADS)
  return jnp.concatenate([q, k, jnp.ones((BRANCH,), F32)])


def _layer_a(x, norm_gain, w_in, w_out, q_gain, k_gain, t5_bias, next_in_proj):
  b, s, d = x.shape
  x2d = x.reshape(b * s, d)
  ones = jnp.ones((BRANCH,), F32)
  reach = A_TQ // 2
  assert all((window // 2) // dilation == reach for window, dilation in DILATED_PAIRS)
  bias = _dilated_bias(t5_bias, reach)
  outs, lses, gated = [], [], None
  for g, (window, dilation) in enumerate(DILATED_PAIRS):
    seq = s // dilation
    assert seq >= A_TK and MAX_DILATION % dilation == 0
    last = g == N_GROUPS - 1
    w_cols = ((g * 3 * BRANCH, 3 * BRANCH),) + (((N_GROUPS * 3 * BRANCH, BRANCH),) if last else ())
    gain = _col_gain(q_gain[g], k_gain[g])
    roles = QKV_ROLES + ((SILU,) if last else ())
    qkv = _in_proj(x2d, b, norm_gain, w_in, w_cols,
                   jnp.concatenate([gain, ones]) if last else gain, roles, dilation)
    n_pairs = max(1, min(N_PAIRS, 4096 // seq))
    o, lse = _dilated_attention(qkv, bias, g, n_pairs)
    outs.append(o)
    lses.append(lse)
    if last:
      assert dilation == MAX_DILATION
      gated = qkv
  y, proj = _merge_out_proj(outs, lses, gated, 3, x2d, w_out.astype(BF16), *next_in_proj)
  return y.reshape(b, s, d), proj


B_ROLES = QKV_ROLES + (SILU,)


def _b_in_proj_args(norm_gain, w_in, q_gain, k_gain):
  gains = jnp.concatenate([_col_gain(q_gain, k_gain), jnp.ones((BRANCH,), F32)])
  return norm_gain, w_in.astype(BF16), gains, B_ROLES


def _layer_b(x, proj, w_out, rpb):
  b, s, d = x.shape
  proj = proj.reshape(b, 4 * N_PAIRS, s, LANES)
  o = _na_attention(proj, _na_bias(rpb))
  y = _out_proj(o, proj, 3, x.reshape(b * s, d), w_out.astype(BF16))
  return y.reshape(b, s, d)


def kernel(x, norm_gain, a_w_in, a_w_out, a_q_gain, a_k_gain, t5_bias,
           b_w_in, b_w_out, b_q_gain, b_k_gain, b_rpb):
  assert norm_gain.shape[0] == 2 and a_w_in.shape[0] == 1 and b_w_in.shape[0] == 1
  x, proj = _layer_a(x, norm_gain[0], a_w_in[0], a_w_out[0], a_q_gain[0], a_k_gain[0], t5_bias,
                     _b_in_proj_args(norm_gain[1], b_w_in[0], b_q_gain[0], b_k_gain[0]))
  return _layer_b(x, proj, b_w_out[0], b_rpb[0])
```

```python
import functools
import math

import numpy as np
import jax
import jax.numpy as jnp
from jax import lax
from jax.experimental import pallas as pl
from jax.experimental.pallas import tpu as pltpu

D_MODEL = 1024
HEAD_DIM = 64
N_HEADS = D_MODEL // HEAD_DIM
BRANCH = N_HEADS * HEAD_DIM
DILATED_PAIRS = ((128, 1), (512, 4), (2048, 16))
N_GROUPS = len(DILATED_PAIRS)
MAX_DILATION = max(d for _, d in DILATED_PAIRS)
T5_BUCKETS = 32
T5_MAX_DISTANCE = 1024
GRID_W = 64
NA_ROWS = 8
NA_COLS = 16
RMS_EPS = 1e-6
MASK_VALUE = -1e30
LOG2E = math.log2(math.e)

LANES = 128
BF16_ROWS = 16
V7X_VMEM_BYTES = 64 * 2 ** 20
PAIR = LANES // HEAD_DIM
N_SLABS = D_MODEL // LANES
N_PAIRS = BRANCH // LANES
DEINTERLEAVE_STRIDE = 4

F32 = jnp.float32
BF16 = jnp.bfloat16


def _vmem_limit(block_bytes, extra_bytes):
  need = 2 * block_bytes + extra_bytes
  return int(min(need + need // 4, V7X_VMEM_BYTES - 8 * 2 ** 20))


def _slab(c):
  return slice(c * LANES, (c + 1) * LANES)


def _head_norm(a, lo, gain):
  a2 = a * a
  s_lo = jnp.sum(jnp.where(lo, a2, 0.0), axis=-1, keepdims=True)
  s_hi = jnp.sum(jnp.where(lo, 0.0, a2), axis=-1, keepdims=True)
  inv = lax.rsqrt(jnp.where(lo, s_lo, s_hi) + HEAD_DIM * RMS_EPS)
  return (a * inv) * gain


def _in_proj_kernel(*refs, dilation, roles, n_w):
  x_refs = refs[:N_SLABS]
  ng_ref = refs[N_SLABS]
  w_pieces = refs[N_SLABS + 1:N_SLABS + 1 + n_w]
  cg_ref, o_ref, hn_ref, w_ref = refs[N_SLABS + 1 + n_w:N_SLABS + 5 + n_w]
  tm = x_refs[0].shape[0]
  seg = tm // dilation

  @pl.when(pl.program_id(0) == 0)
  def _():
    col = 0
    for piece in w_pieces:
      w_ref[:, col:col + piece.shape[1]] = piece[...].astype(BF16)
      col += piece.shape[1]

  two_pass = dilation > DEINTERLEAVE_STRIDE
  if two_pass:
    mid_ref, = refs[N_SLABS + 5 + n_w:]
    inner, outer = DEINTERLEAVE_STRIDE, dilation // DEINTERLEAVE_STRIDE
    for c in range(N_SLABS):
      for b in range(inner):
        mid_ref[c, b] = x_refs[c][pl.ds(b, tm // inner, stride=inner), :]

  for r in range(dilation):
    if dilation == 1:
      xs = [ref[...] for ref in x_refs]
    elif two_pass:
      xs = [mid_ref[c, r % inner, pl.ds(r // inner, seg, stride=outer), :] for c in range(N_SLABS)]
    else:
      xs = [ref[pl.ds(r, seg, stride=dilation), :] for ref in x_refs]
    sq = functools.reduce(lambda a, b: a + b, [x * x for x in xs])
    inv = lax.rsqrt(jnp.sum(sq, axis=-1, keepdims=True) * (1.0 / D_MODEL) + RMS_EPS)
    for c in range(N_SLABS):
      hn_ref[r * seg:(r + 1) * seg, _slab(c)] = ((xs[c] * inv) * ng_ref[:, _slab(c)]).astype(BF16)
  _project(hn_ref, w_ref, cg_ref, o_ref, dilation, roles)


NORM, PLAIN, SILU = "norm", "plain", "silu"


def _project(hn_ref, w_ref, cg_ref, o_ref, dilation, roles):
  tm = hn_ref.shape[0]
  seg = tm // dilation
  lo = lax.broadcasted_iota(jnp.int32, (tm, LANES), 1) < HEAD_DIM
  for cb, role in enumerate(roles):
    acc = jnp.dot(hn_ref[...], w_ref[:, cb * BRANCH:(cb + 1) * BRANCH],
                  preferred_element_type=F32)
    for c in range(N_PAIRS):
      blk = cb * N_PAIRS + c
      a = acc[:, _slab(c)]
      if role == NORM:
        a = _head_norm(a, lo, cg_ref[:, _slab(blk)])
      elif role == SILU:
        a = jax.nn.silu(a)
      y = a.astype(BF16)
      for r in range(dilation):
        o_ref[r, blk] = y[r * seg:(r + 1) * seg]


def _in_proj(x2d, batch, norm_gain, w, w_cols, col_gain, roles, dilation, tm=512):
  n, d = x2d.shape
  c = sum(width for _, width in w_cols)
  s = n // batch
  tiles = s // tm
  seg = tm // dilation
  assert s % tm == 0 and seg % BF16_ROWS == 0 and c == BRANCH * len(roles)
  assert all(first % width == 0 for first, width in w_cols)
  const = lambda i: (0, 0)
  block_bytes = tm * d * 4 + tm * c * 2 + (d + c) * 4
  kern = functools.partial(_in_proj_kernel, dilation=dilation, roles=roles, n_w=len(w_cols))
  return pl.pallas_call(
      kern,
      grid=(n // tm,),
      in_specs=[pl.BlockSpec((tm, LANES), lambda i, cc=cc: (i, cc)) for cc in range(N_SLABS)] + [
          pl.BlockSpec((1, d), const)] + [
          pl.BlockSpec((d, width), lambda i, blk=first // width: (0, blk), pipeline_mode=pl.Buffered(1))
          for first, width in w_cols] + [
          pl.BlockSpec((1, c), const),
      ],
      out_specs=pl.BlockSpec((None, dilation, c // LANES, seg, LANES),
                             lambda i: (i // tiles, 0, 0, i % tiles, 0)),
      out_shape=jax.ShapeDtypeStruct((batch, dilation, c // LANES, s // dilation, LANES), BF16),
      scratch_shapes=[pltpu.VMEM((tm, d), BF16), pltpu.VMEM((d, c), BF16)] + (
          [pltpu.VMEM((N_SLABS, DEINTERLEAVE_STRIDE, tm // DEINTERLEAVE_STRIDE, LANES), F32)]
          if dilation > DEINTERLEAVE_STRIDE else []),
      compiler_params=pltpu.CompilerParams(
          dimension_semantics=("arbitrary",),
          vmem_limit_bytes=_vmem_limit(block_bytes, d * c * 6 + tm * d * 6 + 3 * tm * BRANCH * 4)),
      name=f"in_proj_d{dilation}",
  )(*([x2d] * N_SLABS), norm_gain.reshape(1, d), *([w] * len(w_cols)), col_gain.reshape(1, c))


GROUP = 4


def _divmod_pow2(x, n):
  assert n & (n - 1) == 0
  return lax.shift_right_logical(x, n.bit_length() - 1), x & (n - 1)


def _pipeline(n_groups, stage1, stage2, stage3):
  assert n_groups % 2 == 0 and n_groups >= 2
  stage1(0, 0)
  stage1(1, 1)
  stage2(0, 0)

  def step(j, carry):
    g = 2 * j + 2
    stage1(g, 0)
    stage2(g - 1, 1)
    stage3(g - 2, 0)
    stage1(g + 1, 1)
    stage2(g, 0)
    stage3(g - 1, 1)
    return carry

  lax.fori_loop(0, (n_groups - 2) // 2, step, 0)
  stage2(n_groups - 1, 1)
  stage3(n_groups - 2, 0)
  stage3(n_groups - 1, 1)


def _attention_stages(q_ref, k_ref, v_ref, b_ref, bufs, *, tq, tk, tiles_per_pair, geometry, emit):
  lo = lax.broadcasted_iota(jnp.int32, (tq, LANES), 1) < HEAD_DIM

  def locate(g, u):
    pair, t = _divmod_pow2(jnp.asarray(g, jnp.int32) * GROUP + u, tiles_per_pair)
    return pair, geometry(t)

  def stage1(g, par):
    s_buf = bufs[par][0]
    for u in range(GROUP):
      pair, (q0, k0, var) = locate(g, u)
      q = q_ref[pair, pl.ds(q0, tq), :]
      zero = jnp.zeros_like(q)
      q2 = jnp.concatenate([jnp.where(lo, q, zero), jnp.where(lo, zero, q)], axis=0)
      k = k_ref[pair, pl.ds(k0, tk), :]
      s = lax.dot_general(q2, k, (((1,), (1,)), ((), ())), preferred_element_type=F32)
      s_buf[u] = s + b_ref[var, pair]

  def stage2(g, par):
    s_buf, p_buf, m_buf = bufs[par]
    for u in range(GROUP):
      m = jnp.max(s_buf[u], axis=-1, keepdims=True)
      p_buf[u] = jnp.exp2(s_buf[u] - m).astype(BF16)
      if m_buf is not None:
        m_buf[u] = jnp.where(lo, m[:tq], m[tq:])

  ones = jnp.ones((tk, LANES), BF16)

  def stage3(g, par):
    _, p_buf, m_buf = bufs[par]
    for u in range(GROUP):
      pair, (_, k0, _) = locate(g, u)
      v1 = jnp.concatenate([v_ref[pair, pl.ds(k0, tk), :], ones], axis=1)
      pv = jnp.dot(p_buf[u], v1, preferred_element_type=F32)
      acc = jnp.where(lo, pv[:tq, :LANES], pv[tq:, :LANES])
      l = jnp.where(lo, pv[:tq, LANES:], pv[tq:, LANES:])
      lse = None if m_buf is None else m_buf[u] + jnp.log2(l)
      emit(g, u, acc * (1.0 / l), lse)

  return stage1, stage2, stage3


def _handoff_scratch(tq, tk, with_lse):
  one = [pltpu.VMEM((GROUP, 2 * tq, tk), F32), pltpu.VMEM((GROUP, 2 * tq, tk), BF16)]
  if with_lse:
    one.append(pltpu.VMEM((GROUP, tq, LANES), F32))
  return one + one


def _split_handoff(scratch, with_lse):
  n = 3 if with_lse else 2
  bufs = []
  for par in range(2):
    b = tuple(scratch[par * n:(par + 1) * n])
    bufs.append(b if with_lse else b + (None,))
  return bufs, scratch[2 * n:]


A_TQ = 128
A_TK = 256


def _dilated_attn_kernel(*refs, n_pairs, seq, spread, tiles_per_store, has_prev, has_gate):
  q_ref, k_ref, v_ref, b_ref = refs[:4]
  n_in = 4 + 2 * has_prev + has_gate
  prev_o, prev_l = refs[4:6] if has_prev else (None, None)
  gate_ref = refs[n_in - 1] if has_gate else None
  o_ref = refs[n_in]
  l_ref = None if has_gate else refs[n_in + 1]
  scratch = refs[n_in + (1 if has_gate else 2):]
  bufs, rest = _split_handoff(scratch, True)
  n_tiles = seq // A_TQ

  def deliver(plane, pair, rows, out, lse):
    if has_prev:
      old_l = prev_l[plane, pair, rows, :]
      diff = lse - old_l
      w_old = 1.0 / (1.0 + jnp.exp2(diff))
      out = out + w_old * (prev_o[plane, pair, rows, :].astype(F32) - out)
      if not has_gate:
        lse = jnp.maximum(old_l, lse) + jnp.log2(1.0 + jnp.exp2(-jnp.abs(diff)))
    if has_gate:
      out = out * gate_ref[pair, rows, :].astype(F32)
    else:
      l_ref[plane, pair, rows, :] = lse
    o_ref[plane, pair, rows, :] = out.astype(BF16)
  rows = tiles_per_store * A_TQ
  seg = rows // spread

  def geometry(t):
    q0 = pl.multiple_of(t * A_TQ, A_TQ)
    k0 = pl.multiple_of(jnp.clip(q0 - A_TQ // 2, 0, seq - A_TK), A_TQ // 2)
    var = jnp.where(t == 0, 0, jnp.where(t == n_tiles - 1, 2, 1))
    return q0, k0, var

  def emit(g, u, out, lse):
    pair, t = _divmod_pow2(jnp.asarray(g, jnp.int32) * GROUP + u, n_tiles)
    if spread == 1:
      deliver(0, pair, pl.ds(pl.multiple_of(t * A_TQ, A_TQ), A_TQ), out, lse)
      return
    so_ref, sl_ref = rest[:2]
    slot, w = divmod(u, tiles_per_store)
    so_ref[slot, w * A_TQ:(w + 1) * A_TQ, :] = out
    sl_ref[slot, w * A_TQ:(w + 1) * A_TQ, :] = lse
    if w != tiles_per_store - 1:
      return
    dst_rows = pl.ds(pl.multiple_of(_divmod_pow2(t, tiles_per_store)[0] * seg, seg), seg)
    if spread <= DEINTERLEAVE_STRIDE:
      for kk in range(spread):
        deliver(kk, pair, dst_rows, so_ref[slot, pl.ds(kk, seg, stride=spread), :],
                sl_ref[slot, pl.ds(kk, seg, stride=spread), :])
      return
    inner, outer = DEINTERLEAVE_STRIDE, spread // DEINTERLEAVE_STRIDE
    mo_ref, ml_ref = rest[2:4]
    for b in range(inner):
      mo_ref[slot, b] = so_ref[slot, pl.ds(b, rows // inner, stride=inner), :]
      ml_ref[slot, b] = sl_ref[slot, pl.ds(b, rows // inner, stride=inner), :]
    for b in range(inner):
      for a in range(outer):
        deliver(a * inner + b, pair, dst_rows, mo_ref[slot, b, pl.ds(a, seg, stride=outer), :],
                ml_ref[slot, b, pl.ds(a, seg, stride=outer), :])

  stages = _attention_stages(q_ref, k_ref, v_ref, b_ref, bufs, tq=A_TQ, tk=A_TK,
                             tiles_per_pair=n_tiles, geometry=geometry, emit=emit)
  _pipeline(n_pairs * n_tiles // GROUP, *stages)


def _t5_bucket(rel):
  half = T5_BUCKETS // 2
  max_exact = half // 2
  ret = jnp.where(rel > 0, half, 0)
  n = jnp.abs(rel)
  nf = jnp.maximum(n, 1).astype(jnp.float32)
  large = max_exact + (jnp.log(nf / max_exact) / math.log(T5_MAX_DISTANCE / max_exact)
                       * (half - max_exact)).astype(jnp.int32)
  large = jnp.minimum(large, half - 1)
  return ret + jnp.where(n < max_exact, n, large)


def _stack_pairs(bias):
  h, tq, tk = bias.shape[-3:]
  return bias.reshape(bias.shape[:-3] + (h // PAIR, PAIR * tq, tk))


def _dilated_bias(t5_bias, reach):
  rel = np.arange(-reach, reach + 1)
  bands = []
  for g, (_, dilation) in enumerate(DILATED_PAIRS):
    table = t5_bias[g * N_HEADS:(g + 1) * N_HEADS]
    bands.append(table[:, _t5_bucket(jnp.asarray(rel * dilation, dtype=jnp.int32))])
  band = jnp.stack(bands).astype(F32) * LOG2E
  n_slots = 2 * reach + 2
  band = jnp.pad(band, ((0, 0), (0, 0), (0, 1)), constant_values=MASK_VALUE)
  a = np.arange(A_TQ)[None, :, None]
  jj = np.arange(A_TK)[None, None, :]
  delta = np.array([0, -reach, -2 * reach])[:, None, None]
  off = delta + jj - a
  slot = np.where(np.abs(off) <= reach, off + reach, n_slots - 1)
  onehot = (jnp.asarray(slot, jnp.int32)[..., None] == jnp.arange(n_slots, dtype=jnp.int32)).astype(F32)
  bias = jnp.einsum("vajr,ghr->gvhaj", onehot, band, precision=lax.Precision.HIGHEST)
  return _stack_pairs(bias)


def _dilated_attention(qkv, bias, group, n_pairs, prev=None, gated=False):
  b, d, _, seq, _ = qkv.shape
  spread = MAX_DILATION // d
  planes = seq // spread
  n_hb = N_PAIRS // n_pairs
  n_tiles = seq // A_TQ
  tiles_per_store = max(1, BF16_ROWS * spread // A_TQ)
  assert (tiles_per_store * A_TQ // spread) % BF16_ROWS == 0 and GROUP % tiles_per_store == 0
  assert n_tiles % tiles_per_store == 0 and (n_pairs * n_tiles) % (2 * GROUP) == 0
  assert n_tiles % GROUP == 0 or GROUP % n_tiles == 0

  def in_map(t):
    return lambda hb, bi, r: (bi, r, t * n_hb + hb, 0, 0)

  out_map = lambda hb, bi, r: (bi, 0, r, hb, 0, 0)
  in_blk = (None, None, n_pairs, seq, LANES)
  out_blk = (None, spread, None, n_pairs, planes, LANES)
  block_bytes = n_pairs * seq * LANES * (3 * 2 + 2 + 4) + 3 * n_pairs * PAIR * A_TQ * A_TK * 4
  scratch = _handoff_scratch(A_TQ, A_TK, True)
  if spread > 1:
    slots, rows = GROUP // tiles_per_store, tiles_per_store * A_TQ
    scratch += [pltpu.VMEM((slots, rows, LANES), F32)] * 2
  if spread > DEINTERLEAVE_STRIDE:
    assert spread % DEINTERLEAVE_STRIDE == 0
    scratch += [pltpu.VMEM((slots, DEINTERLEAVE_STRIDE, rows // DEINTERLEAVE_STRIDE, LANES), F32)] * 2
  kern = functools.partial(_dilated_attn_kernel, n_pairs=n_pairs, seq=seq, spread=spread,
                           tiles_per_store=tiles_per_store, has_prev=prev is not None, has_gate=gated)
  planes_shape = (b, spread, d, N_PAIRS, planes, LANES)
  out_spec = pl.BlockSpec(out_blk, out_map)
  extra_in, extra_specs = [], []
  if prev is not None:
    extra_in += [a.reshape(planes_shape) for a in prev]
    extra_specs += [out_spec, out_spec]
    block_bytes += n_pairs * seq * LANES * (2 + 4)
  if gated:
    extra_in.append(qkv)
    extra_specs.append(pl.BlockSpec(in_blk, in_map(3)))
    block_bytes += n_pairs * seq * LANES * 2
  res = pl.pallas_call(
      kern,
      grid=(n_hb, b, d),
      in_specs=[
          pl.BlockSpec(in_blk, in_map(0)),
          pl.BlockSpec(in_blk, in_map(1)),
          pl.BlockSpec(in_blk, in_map(2)),
          pl.BlockSpec((None, 3, n_pairs, PAIR * A_TQ, A_TK), lambda hb, bi, r: (group, 0, hb, 0, 0)),
      ] + extra_specs,
      out_specs=[out_spec] if gated else [out_spec, out_spec],
      out_shape=[jax.ShapeDtypeStruct(planes_shape, BF16)] + (
          [] if gated else [jax.ShapeDtypeStruct(planes_shape, F32)]),
      scratch_shapes=scratch,
      compiler_params=pltpu.CompilerParams(
          dimension_semantics=("parallel", "parallel", "parallel"),
          vmem_limit_bytes=_vmem_limit(block_bytes, 12 * 2 ** 20)),
      name=f"dilated_attn_d{d}",
  )(qkv, qkv, qkv, bias, *extra_in)
  shape = (b, MAX_DILATION, N_PAIRS, planes, LANES)
  return tuple(a.reshape(shape) for a in res)


NA_TK = NA_ROWS * GRID_W


def _na_attn_kernel(q_ref, k_ref, v_ref, b_ref, o_ref, *scratch, rows):
  bufs, _ = _split_handoff(scratch, False)

  def geometry(r):
    rs = jnp.clip(r - NA_ROWS // 2, 0, rows - NA_ROWS)
    return (pl.multiple_of(r * GRID_W, GRID_W), pl.multiple_of(rs * GRID_W, GRID_W),
            rs - r + NA_ROWS - 1)

  def emit(g, u, out, _):
    q0 = pl.multiple_of((jnp.asarray(g, jnp.int32) * GROUP + u) * GRID_W, GRID_W)
    o_ref[0, pl.ds(q0, GRID_W), :] = out.astype(BF16)

  stages = _attention_stages(q_ref, k_ref, v_ref, b_ref, bufs, tq=GRID_W, tk=NA_TK,
                             tiles_per_pair=rows, geometry=geometry, emit=emit)
  _pipeline(rows // GROUP, *stages)


def _na_bias(rpb):
  qc = np.arange(GRID_W)[:, None]
  kc = np.arange(GRID_W)[None, :]
  cstart = np.clip(qc - NA_COLS // 2, 0, GRID_W - NA_COLS)
  valid = (kc >= cstart) & (kc < cstart + NA_COLS)
  col_idx = np.clip(kc - qc, -(NA_COLS - 1), NA_COLS - 1) + NA_COLS - 1
  col_onehot = (col_idx[..., None] == np.arange(2 * NA_COLS - 1)).astype(np.float32)
  cols = jnp.einsum("qkc,hjc->hqjk", col_onehot, rpb.astype(F32) * LOG2E,
                    precision=lax.Precision.HIGHEST)
  cols = jnp.where(jnp.asarray(valid)[None, :, None, :], cols, MASK_VALUE)
  cols = cols.reshape(N_HEADS, GRID_W, (2 * NA_ROWS - 1) * GRID_W)
  return _stack_pairs(jnp.stack([cols[:, :, u * GRID_W:u * GRID_W + NA_TK] for u in range(NA_ROWS)]))


def _na_attention(proj, bias):
  b, _, s, _ = proj.shape
  rows = s // GRID_W
  blk = (None, 1, s, LANES)
  assert rows % (2 * GROUP) == 0
  block_bytes = s * LANES * 2 * 4 + NA_ROWS * PAIR * GRID_W * NA_TK * 4
  return pl.pallas_call(
      functools.partial(_na_attn_kernel, rows=rows),
      grid=(N_PAIRS, b),
      in_specs=[
          pl.BlockSpec(blk, lambda hb, bi: (bi, hb, 0, 0)),
          pl.BlockSpec(blk, lambda hb, bi: (bi, N_PAIRS + hb, 0, 0)),
          pl.BlockSpec(blk, lambda hb, bi: (bi, 2 * N_PAIRS + hb, 0, 0)),
          pl.BlockSpec((NA_ROWS, 1, PAIR * GRID_W, NA_TK), lambda hb, bi: (0, hb, 0, 0)),
      ],
      out_specs=pl.BlockSpec(blk, lambda hb, bi: (bi, hb, 0, 0)),
      out_shape=jax.ShapeDtypeStruct((b, N_PAIRS, s, LANES), BF16),
      scratch_shapes=_handoff_scratch(GRID_W, NA_TK, False),
      compiler_params=pltpu.CompilerParams(
          dimension_semantics=("parallel", "parallel"),
          vmem_limit_bytes=_vmem_limit(block_bytes, 12 * 2 ** 20)),
      name="na_attn",
  )(proj, proj, proj, bias)


def _out_in_proj_kernel(y_ref, x_ref, w_ref, ng_ref, w2_ref, cg2_ref, out_ref, proj_ref,
                        y_slab, y_bf, hn_ref, *, next_roles):
  seg = y_ref.shape[2]

  @pl.when(pl.program_id(0) == 0)
  def _():
    hn_ref[...] = jnp.zeros_like(hn_ref)

  _project(hn_ref, w2_ref, cg2_ref, proj_ref, 1, next_roles)
  for r in range(MAX_DILATION):
    for c in range(N_PAIRS):
      y_slab[c, pl.ds(r, seg, stride=MAX_DILATION), :] = y_ref[r, c].astype(F32)
  for c in range(N_PAIRS):
    y_bf[:, _slab(c)] = y_slab[c].astype(BF16)
  xn = x_ref[...] + jnp.dot(y_bf[...], w_ref[...], preferred_element_type=F32)
  out_ref[...] = xn
  inv = lax.rsqrt(jnp.mean(xn * xn, axis=-1, keepdims=True) + RMS_EPS)
  hn_ref[...] = ((xn * inv) * ng_ref[...]).astype(BF16)


def _out_in_proj(y, x2d, w_bf16, next_norm_gain, next_w_bf16, next_col_gain, next_roles, tm=256):
  n, d = x2d.shape
  b, _, _, planes, _ = y.shape
  s = planes * MAX_DILATION
  c2 = next_w_bf16.shape[1]
  seg = tm // MAX_DILATION
  tiles = planes // seg
  assert seg % BF16_ROWS == 0 and planes % seg == 0 and c2 == BRANCH * len(next_roles)
  res_blk = (None, MAX_DILATION, N_PAIRS, seg, LANES)
  n_steps = n // tm
  merged = lambda i: jnp.minimum(i, n_steps - 1)
  projected = lambda i: jnp.maximum(i - 1, 0)
  res_map = lambda i: (merged(i) // tiles, 0, 0, merged(i) % tiles, 0)
  const = lambda i: (0, 0)
  block_bytes = tm * BRANCH * 2 + 2 * tm * d * 4 + tm * c2 * 2
  return pl.pallas_call(
      functools.partial(_out_in_proj_kernel, next_roles=next_roles),
      grid=(n_steps + 1,),
      in_specs=[
          pl.BlockSpec(res_blk, res_map),
          pl.BlockSpec((tm, d), lambda i: (merged(i), 0)),
          pl.BlockSpec((BRANCH, d), const, pipeline_mode=pl.Buffered(1)),
          pl.BlockSpec((1, d), const),
          pl.BlockSpec((d, c2), const, pipeline_mode=pl.Buffered(1)),
          pl.BlockSpec((1, c2), const),
      ],
      out_specs=[pl.BlockSpec((tm, d), lambda i: (merged(i), 0)),
                 pl.BlockSpec((None, 1, c2 // LANES, tm, LANES),
                              lambda i: (projected(i) // tiles, 0, 0, projected(i) % tiles, 0))],
      out_shape=[jax.ShapeDtypeStruct((n, d), F32),
                 jax.ShapeDtypeStruct((b, 1, c2 // LANES, s, LANES), BF16)],
      scratch_shapes=[pltpu.VMEM((N_PAIRS, tm, LANES), F32), pltpu.VMEM((tm, BRANCH), BF16),
                      pltpu.VMEM((tm, d), BF16)],
      compiler_params=pltpu.CompilerParams(
          dimension_semantics=("arbitrary",),
          vmem_limit_bytes=_vmem_limit(block_bytes, (BRANCH * d + d * c2) * 2 + tm * BRANCH * 8
                                       + 3 * tm * BRANCH * 4 + 4 * 2 ** 20)),
      name="out_in_proj",
  )(y, x2d, w_bf16, next_norm_gain.reshape(1, d), next_w_bf16, next_col_gain.reshape(1, c2))


def _out_proj_kernel(o_ref, g_ref, x_ref, w_ref, out_ref, y_bf):
  for c in range(N_PAIRS):
    y = o_ref[c].astype(F32) * g_ref[c].astype(F32)
    y_bf[:, _slab(c)] = y.astype(BF16)
  out_ref[...] = x_ref[...] + jnp.dot(y_bf[...], w_ref[...], preferred_element_type=F32)


def _out_proj(o, proj, gate_block, x2d, w_bf16, tm=1024):
  n, d = x2d.shape
  b, _, s, _ = o.shape
  tiles = s // tm
  blk = (None, N_PAIRS, tm, LANES)
  block_bytes = tm * BRANCH * 4 + 2 * tm * d * 4
  return pl.pallas_call(
      _out_proj_kernel,
      grid=(n // tm,),
      in_specs=[
          pl.BlockSpec(blk, lambda i: (i // tiles, 0, i % tiles, 0)),
          pl.BlockSpec(blk, lambda i: (i // tiles, gate_block, i % tiles, 0)),
          pl.BlockSpec((tm, d), lambda i: (i, 0)),
          pl.BlockSpec((BRANCH, d), lambda i: (0, 0), pipeline_mode=pl.Buffered(1)),
      ],
      out_specs=pl.BlockSpec((tm, d), lambda i: (i, 0)),
      out_shape=jax.ShapeDtypeStruct((n, d), F32),
      scratch_shapes=[pltpu.VMEM((tm, BRANCH), BF16)],
      compiler_params=pltpu.CompilerParams(
          dimension_semantics=("parallel",),
          vmem_limit_bytes=_vmem_limit(block_bytes, BRANCH * d * 2 + tm * BRANCH * 2 + 2 * tm * d * 4)),
      name="out_proj",
  )(o, proj, x2d, w_bf16)


QKV_ROLES = (NORM, NORM, PLAIN)


def _col_gain(q_gain, k_gain):
  q = jnp.tile(q_gain.astype(F32) * LOG2E, N_HEADS)
  k = jnp.tile(k_gain.astype(F32) * HEAD_DIM ** 0.5, N_HEADS)
  return jnp.concatenate([q, k, jnp.ones((BRANCH,), F32)])


def _layer_a(x, norm_gain, w_in, w_out, q_gain, k_gain, t5_bias, next_in_proj):
  b, s, d = x.shape
  x2d = x.reshape(b * s, d)
  ones = jnp.ones((BRANCH,), F32)
  reach = A_TQ // 2
  assert all((window // 2) // dilation == reach for window, dilation in DILATED_PAIRS)
  bias = _dilated_bias(t5_bias, reach)
  merged = None
  for g, (window, dilation) in enumerate(DILATED_PAIRS):
    seq = s // dilation
    assert seq >= A_TK and MAX_DILATION % dilation == 0
    last = g == N_GROUPS - 1
    w_cols = ((g * 3 * BRANCH, 3 * BRANCH),) + (((N_GROUPS * 3 * BRANCH, BRANCH),) if last else ())
    gain = _col_gain(q_gain[g], k_gain[g])
    roles = QKV_ROLES + ((SILU,) if last else ())
    qkv = _in_proj(x2d, b, norm_gain, w_in, w_cols,
                   jnp.concatenate([gain, ones]) if last else gain, roles, dilation)
    n_pairs = max(1, min(N_PAIRS, 4096 // seq))
    assert not last or dilation == MAX_DILATION
    merged = _dilated_attention(qkv, bias, g, n_pairs, prev=merged, gated=last)
  y, = merged
  x_new, proj = _out_in_proj(y, x2d, w_out.astype(BF16), *next_in_proj)
  return x_new.reshape(b, s, d), proj


B_ROLES = QKV_ROLES + (SILU,)


def _b_in_proj_args(norm_gain, w_in, q_gain, k_gain):
  gains = jnp.concatenate([_col_gain(q_gain, k_gain), jnp.ones((BRANCH,), F32)])
  return norm_gain, w_in.astype(BF16), gains, B_ROLES


def _layer_b(x, proj, w_out, rpb):
  b, s, d = x.shape
  proj = proj.reshape(b, 4 * N_PAIRS, s, LANES)
  o = _na_attention(proj, _na_bias(rpb))
  y = _out_proj(o, proj, 3, x.reshape(b * s, d), w_out.astype(BF16))
  return y.reshape(b, s, d)


def kernel(x, norm_gain, a_w_in, a_w_out, a_q_gain, a_k_gain, t5_bias,
           b_w_in, b_w_out, b_q_gain, b_k_gain, b_rpb):
  assert norm_gain.shape[0] == 2 and a_w_in.shape[0] == 1 and b_w_in.shape[0] == 1
  x, proj = _layer_a(x, norm_gain[0], a_w_in[0], a_w_out[0], a_q_gain[0], a_k_gain[0], t5_bias,
                     _b_in_proj_args(norm_gain[1], b_w_in[0], b_q_gain[0], b_k_gain[0]))
  return _layer_b(x, proj, b_w_out[0], b_rpb[0])
```
